```python
import math
import jax
import jax.numpy as jnp
from jax import lax
import numpy as np

D_MODEL = 1024
BATCH = 16
SEQ = 4096
DEPTH = 1

MIX_WIDTH = D_MODEL
ATTN_WIDTH = MIX_WIDTH // 2
SSM_WIDTH = MIX_WIDTH - ATTN_WIDTH
ATTN_HEADS = 4
ATTN_HEAD_DIM = ATTN_WIDTH // (2 * ATTN_HEADS)
SSM_GROUP = 16
SSM_GROUPS = SSM_WIDTH // SSM_GROUP
SSM_STATE = 64
DT_MIN = 1e-3
DT_MAX = 1e-1
D_FF = 2816
CONV_WIDTH = 3
REL_BUCKETS = 32
REL_MAX_DIST = 128
Q_BLOCK = 128
LN_EPS = 1e-5
NEG_INF = -1e30
DEEPNORM_ALPHA = (2 * DEPTH) ** 0.25
DEEPNORM_BETA = (8 * DEPTH) ** -0.25

kernel_name = 'hybrid_diffattn_s5_convffn_deepnorm_adaln'


def _layernorm(x, g=None, b=None):
    xf = x.astype(jnp.float32)
    mu = jnp.mean(xf, axis=-1, keepdims=True)
    var = jnp.mean(jnp.square(xf - mu), axis=-1, keepdims=True)
    y = (xf - mu) * lax.rsqrt(var + LN_EPS)
    if g is not None:
        y = y * g.astype(jnp.float32) + b.astype(jnp.float32)
    return y.astype(x.dtype)


def _rmsnorm(x, g):
    xf = x.astype(jnp.float32)
    y = xf * lax.rsqrt(jnp.mean(jnp.square(xf), axis=-1, keepdims=True) + LN_EPS) * g.astype(jnp.float32)
    return y.astype(x.dtype)


def _t5_bucket(dist):
    max_exact = REL_BUCKETS // 2
    distf = jnp.maximum(dist, 1).astype(jnp.float32)
    large = max_exact + (jnp.log(distf / max_exact) / math.log(REL_MAX_DIST / max_exact)
                         * (REL_BUCKETS - max_exact)).astype(jnp.int32)
    large = jnp.minimum(large, REL_BUCKETS - 1)
    return jnp.where(dist < max_exact, dist, large)


def _diff_attention(q, k, v, lam, rel_bias):
    seq = q.shape[3]
    scale = ATTN_HEAD_DIM ** -0.5
    table = rel_bias.astype(jnp.float32)
    outs = []
    for start in range(0, seq, Q_BLOCK):
        end = start + Q_BLOCK
        qb = q[:, :, :, start:end]
        kb = k[:, :, :, :end]
        vb = v[:, :, :end]
        logits = jnp.einsum('bhmqd,bhmkd->bhmqk', qb, kb).astype(jnp.float32) * scale
        dist = jnp.arange(start, end)[:, None] - jnp.arange(end)[None, :]
        bias = jnp.transpose(table[_t5_bucket(jnp.maximum(dist, 0))], (2, 0, 1))
        logits = logits + bias[None, :, None]
        logits = jnp.where(dist >= 0, logits, NEG_INF)
        p = jax.nn.softmax(logits, axis=-1)
        w = p[:, :, 0] - lam * p[:, :, 1]
        outs.append(jnp.einsum('bhqk,bhke->bhqe', w.astype(v.dtype), vb))
    return jnp.concatenate(outs, axis=2)


def _complex_affine_combine(e1, e2):
    a1r, a1i, b1r, b1i = e1
    a2r, a2i, b2r, b2i = e2
    ar = a2r * a1r - a2i * a1i
    ai = a2r * a1i + a2i * a1r
    br = a2r * b1r - a2i * b1i + b2r
    bi = a2r * b1i + a2i * b1r + b2i
    return (ar, ai, br, bi)


def _s5_ssm(s, a_re, a_im, b_re, b_im, c_re, c_im, d_skip, log_dt):
    f32 = jnp.float32
    bsz, seq, _ = s.shape
    sf = s.astype(f32)
    ug = sf.reshape(bsz, seq, SSM_GROUPS, SSM_GROUP)
    dt = jnp.exp(log_dt.astype(f32))[:, None]
    ar = a_re.astype(f32)
    ai = a_im.astype(f32)
    mag = jnp.exp(ar * dt)
    ang = ai * dt
    abar_re = mag * jnp.cos(ang)
    abar_im = mag * jnp.sin(ang)
    den = ar * ar + ai * ai
    fr = ((abar_re - 1.0) * ar + abar_im * ai) / den
    fi = (abar_im * ar - (abar_re - 1.0) * ai) / den
    br = b_re.astype(f32)
    bi = b_im.astype(f32)
    bbar_re = fr[..., None] * br - fi[..., None] * bi
    bbar_im = fr[..., None] * bi + fi[..., None] * br
    bu_re = jnp.einsum('blgh,gph->blgp', ug, bbar_re)
    bu_im = jnp.einsum('blgh,gph->blgp', ug, bbar_im)
    a_seq_re = jnp.broadcast_to(abar_re, (1, seq, SSM_GROUPS, SSM_STATE))
    a_seq_im = jnp.broadcast_to(abar_im, (1, seq, SSM_GROUPS, SSM_STATE))
    _, _, h_re, h_im = lax.associative_scan(
        _complex_affine_combine, (a_seq_re, a_seq_im, bu_re, bu_im), axis=1)
    y = (jnp.einsum('blgp,ghp->blgh', h_re, c_re.astype(f32))
         - jnp.einsum('blgp,ghp->blgh', h_im, c_im.astype(f32)))
    return y.reshape(bsz, seq, SSM_WIDTH) + d_skip.astype(f32) * sf


def _hybrid_mixer(u, w_in, lam_q1, lam_k1, lam_q2, lam_k2, subln_g,
                  ssm_a_re, ssm_a_im, ssm_b_re, ssm_b_im, ssm_c_re, ssm_c_im,
                  ssm_d, ssm_log_dt, w_glu, b_glu, w_out, rel_bias, lam_init):
    f32 = jnp.float32
    bsz, seq, _ = u.shape
    proj = u @ w_in
    q, k, v, s = jnp.split(proj, [ATTN_WIDTH, 2 * ATTN_WIDTH, 3 * ATTN_WIDTH], axis=-1)
    q = q.reshape(bsz, seq, ATTN_HEADS, 2, ATTN_HEAD_DIM).transpose(0, 2, 3, 1, 4)
    k = k.reshape(bsz, seq, ATTN_HEADS, 2, ATTN_HEAD_DIM).transpose(0, 2, 3, 1, 4)
    v = v.reshape(bsz, seq, ATTN_HEADS, 2 * ATTN_HEAD_DIM).transpose(0, 2, 1, 3)
    lam = (jnp.exp(jnp.sum(lam_q1.astype(f32) * lam_k1.astype(f32)))
           - jnp.exp(jnp.sum(lam_q2.astype(f32) * lam_k2.astype(f32))) + lam_init)
    o = _diff_attention(q, k, v, lam, rel_bias)
    o = _rmsnorm(o, subln_g) * (1.0 - lam_init)
    o = o.transpose(0, 2, 1, 3).reshape(bsz, seq, ATTN_WIDTH)
    y = jax.nn.gelu(_s5_ssm(s, ssm_a_re, ssm_a_im, ssm_b_re, ssm_b_im,
                            ssm_c_re, ssm_c_im, ssm_d, ssm_log_dt))
    y = (y * jax.nn.sigmoid(y @ w_glu.astype(f32) + b_glu.astype(f32))).astype(u.dtype)
    return jnp.concatenate([o, y], axis=-1) @ w_out


def _conv_ffn(u, w_up, conv_w, conv_b, w_down):
    h = u @ w_up
    ch = h.shape[-1]
    h = lax.conv_general_dilated(h, conv_w[:, None, :], window_strides=(1,),
                                 padding=[(CONV_WIDTH - 1, 0)],
                                 dimension_numbers=('NWC', 'WIO', 'NWC'),
                                 feature_group_count=ch) + conv_b
    val, gate = jnp.split(h, 2, axis=-1)
    return (jax.nn.silu(gate) * val) @ w_down


def setup_inputs(seed: int = 0) -> dict:
    key = jax.random.key(seed)
    ks = jax.random.split(key, 32)
    f32 = jnp.float32

    def nrm(k, shape, std):
        return jax.random.normal(k, shape, f32) * std

    x = nrm(ks[0], (BATCH, SEQ, D_MODEL), 1.0)
    c = nrm(ks[1], (BATCH, D_MODEL), 1.0)
    w_ada = nrm(ks[2], (DEPTH, D_MODEL, 6 * D_MODEL), 0.5 * D_MODEL ** -0.5)
    b_ada = nrm(ks[3], (DEPTH, 6 * D_MODEL), 0.02)
    w_qk = nrm(ks[4], (DEPTH, D_MODEL, 2 * ATTN_WIDTH), D_MODEL ** -0.5)
    w_v = nrm(ks[5], (DEPTH, D_MODEL, ATTN_WIDTH), DEEPNORM_BETA * D_MODEL ** -0.5)
    w_s = nrm(ks[6], (DEPTH, D_MODEL, SSM_WIDTH), D_MODEL ** -0.5)
    w_in = jnp.concatenate([w_qk, w_v, w_s], axis=-1)
    lam_q1 = nrm(ks[7], (DEPTH, ATTN_HEAD_DIM), 0.1)
    lam_k1 = nrm(ks[8], (DEPTH, ATTN_HEAD_DIM), 0.1)
    lam_q2 = nrm(ks[9], (DEPTH, ATTN_HEAD_DIM), 0.1)
    lam_k2 = nrm(ks[10], (DEPTH, ATTN_HEAD_DIM), 0.1)
    subln_g = 1.0 + nrm(ks[11], (DEPTH, 2 * ATTN_HEAD_DIM), 0.02)
    ssm_a_re = -0.5 * jnp.exp(nrm(ks[12], (DEPTH, SSM_GROUPS, SSM_STATE), 0.01))
    ssm_a_im = (math.pi * jnp.arange(SSM_STATE, dtype=f32)[None, None, :]
                + nrm(ks[13], (DEPTH, SSM_GROUPS, SSM_STATE), 0.01))
    ssm_b_re = nrm(ks[14], (DEPTH, SSM_GROUPS, SSM_STATE, SSM_GROUP), (2 * SSM_GROUP) ** -0.5)
    ssm_b_im = nrm(ks[15], (DEPTH, SSM_GROUPS, SSM_STATE, SSM_GROUP), (2 * SSM_GROUP) ** -0.5)
    ssm_c_re = nrm(ks[16], (DEPTH, SSM_GROUPS, SSM_GROUP, SSM_STATE), (2 * SSM_STATE) ** -0.5)
    ssm_c_im = nrm(ks[17], (DEPTH, SSM_GROUPS, SSM_GROUP, SSM_STATE), (2 * SSM_STATE) ** -0.5)
    ssm_d = nrm(ks[18], (DEPTH, SSM_WIDTH), 1.0)
    ssm_log_dt = jax.random.uniform(ks[19], (DEPTH, SSM_GROUPS), f32,
                                    math.log(DT_MIN), math.log(DT_MAX))
    w_glu = nrm(ks[20], (DEPTH, SSM_WIDTH, SSM_WIDTH), SSM_WIDTH ** -0.5)
    b_glu = nrm(ks[21], (DEPTH, SSM_WIDTH), 0.02)
    w_out = nrm(ks[22], (DEPTH, MIX_WIDTH, D_MODEL), DEEPNORM_BETA * MIX_WIDTH ** -0.5)
    ln1_g = 1.0 + nrm(ks[23], (DEPTH, D_MODEL), 0.02)
    ln1_b = nrm(ks[24], (DEPTH, D_MODEL), 0.02)
    w_up = nrm(ks[25], (DEPTH, D_MODEL, 2 * D_FF), DEEPNORM_BETA * D_MODEL ** -0.5)
    conv_w = nrm(ks[26], (DEPTH, CONV_WIDTH, 2 * D_FF), CONV_WIDTH ** -0.5)
    conv_b = nrm(ks[27], (DEPTH, 2 * D_FF), 0.02)
    w_down = nrm(ks[28], (DEPTH, D_FF, D_MODEL), DEEPNORM_BETA * D_FF ** -0.5)
    ln2_g = 1.0 + nrm(ks[29], (DEPTH, D_MODEL), 0.02)
    ln2_b = nrm(ks[30], (DEPTH, D_MODEL), 0.02)
    rel_bias = nrm(ks[31], (REL_BUCKETS, ATTN_HEADS), 0.3)
    return {'x': x, 'c': c, 'w_ada': w_ada, 'b_ada': b_ada, 'w_in': w_in,
            'lam_q1': lam_q1, 'lam_k1': lam_k1, 'lam_q2': lam_q2, 'lam_k2': lam_k2,
            'subln_g': subln_g, 'ssm_a_re': ssm_a_re, 'ssm_a_im': ssm_a_im,
            'ssm_b_re': ssm_b_re, 'ssm_b_im': ssm_b_im, 'ssm_c_re': ssm_c_re,
            'ssm_c_im': ssm_c_im, 'ssm_d': ssm_d, 'ssm_log_dt': ssm_log_dt,
            'w_glu': w_glu, 'b_glu': b_glu, 'w_out': w_out, 'ln1_g': ln1_g, 'ln1_b': ln1_b,
            'w_up': w_up, 'conv_w': conv_w, 'conv_b': conv_b, 'w_down': w_down,
            'ln2_g': ln2_g, 'ln2_b': ln2_b, 'rel_bias': rel_bias}


def reference(x, c, w_ada, b_ada, w_in, lam_q1, lam_k1, lam_q2, lam_k2, subln_g,
              ssm_a_re, ssm_a_im, ssm_b_re, ssm_b_im, ssm_c_re, ssm_c_im, ssm_d,
              ssm_log_dt, w_glu, b_glu, w_out, ln1_g, ln1_b, w_up, conv_w, conv_b,
              w_down, ln2_g, ln2_b, rel_bias):
    c_act = jax.nn.silu(c)
    for layer in range(DEPTH):
        lam_init = 0.8 - 0.6 * math.exp(-0.3 * layer)
        mod = c_act @ w_ada[layer] + b_ada[layer]
        shift1, scale1, gate1, shift2, scale2, gate2 = [
            m[:, None, :] for m in jnp.split(mod, 6, axis=-1)]
        u = _layernorm(x) * (1.0 + scale1) + shift1
        m = _hybrid_mixer(u, w_in[layer], lam_q1[layer], lam_k1[layer], lam_q2[layer],
                          lam_k2[layer], subln_g[layer], ssm_a_re[layer], ssm_a_im[layer],
                          ssm_b_re[layer], ssm_b_im[layer], ssm_c_re[layer], ssm_c_im[layer],
                          ssm_d[layer], ssm_log_dt[layer], w_glu[layer], b_glu[layer],
                          w_out[layer], rel_bias, lam_init)
        x = _layernorm(DEEPNORM_ALPHA * x + gate1 * m, ln1_g[layer], ln1_b[layer])
        u = _layernorm(x) * (1.0 + scale2) + shift2
        f = _conv_ffn(u, w_up[layer], conv_w[layer], conv_b[layer], w_down[layer])
        x = _layernorm(DEEPNORM_ALPHA * x + gate2 * f, ln2_g[layer], ln2_b[layer])
    return x
```

```python
import functools
import math

import numpy as np
import jax
import jax.numpy as jnp
from jax import lax
from jax.experimental import pallas as pl
from jax.experimental.pallas import tpu as pltpu

F32 = jnp.float32
BF16 = jnp.bfloat16

D_MODEL = 1024
BATCH = 16
SEQ = 4096
ATTN_WIDTH = 512
SSM_WIDTH = 512
ATTN_HEADS = 4
HEAD_DIM = 64
HEAD_COLS = 2 * HEAD_DIM
SSM_GROUP = 16
SSM_GROUPS = 32
SSM_STATE = 64
D_FF = 2816
REL_BUCKETS = 32
REL_MAX_DIST = 128
LN_EPS = 1e-5
NEG_INF = -1e30
DEPTH = 1
DEEPNORM_ALPHA = (2 * DEPTH) ** 0.25

LANES = 128
SUBLANES = 8
MXU_DIM = 256
VMEM_LIMIT_BYTES = 56 * 1024 * 1024

PROJ_ROWS = 512
ATTN_TQ = 256
ATTN_TK = 256
SSM_CHUNK = 16
SSM_ROWS = 16
SSM_HALF = SSM_WIDTH // 2
SSM_HALF_STATE = (SSM_GROUPS // 2) * SSM_STATE
FFN_ROWS = 512
FFN_COLS = 256
FFN_STEPS = D_FF // FFN_COLS


def _layernorm(x):
    mu = jnp.mean(x, axis=-1, keepdims=True)
    xc = x - mu
    var = jnp.mean(xc * xc, axis=-1, keepdims=True)
    return xc * lax.rsqrt(var + LN_EPS)


def _mod_kernel(c_ref, w_ref, b_ref, o_ref):
    c = c_ref[...]
    act = c * jax.nn.sigmoid(c)
    o_ref[...] = jnp.dot(act, w_ref[...], precision=lax.Precision.HIGHEST,
                         preferred_element_type=F32) + b_ref[...]


def _modulation(c, w_ada, b_ada):
    n = w_ada.shape[1]
    blk = D_MODEL
    return pl.pallas_call(
        _mod_kernel,
        out_shape=jax.ShapeDtypeStruct((BATCH, n), F32),
        grid=(n // blk,),
        in_specs=[pl.BlockSpec((BATCH, D_MODEL), lambda i: (0, 0)),
                  pl.BlockSpec((D_MODEL, blk), lambda i: (0, i)),
                  pl.BlockSpec((1, blk), lambda i: (0, i))],
        out_specs=pl.BlockSpec((BATCH, blk), lambda i: (0, i)),
        name="adaln_mod",
    )(c, w_ada, b_ada.reshape(1, n))


def _inproj_kernel(x_ref, sh_ref, sc_ref, w_ref, qkv_ref, s_ref):
    u = _layernorm(x_ref[0]) * (1.0 + sc_ref[0]) + sh_ref[0]
    ub = u.astype(BF16)
    scale = HEAD_DIM ** -0.5
    for cblk in range(4):
        lo = cblk * 512
        p = jnp.dot(ub, w_ref[:, lo:lo + 512], preferred_element_type=F32)
        if cblk == 0:
            qkv_ref[0, :, 0:512] = (p * scale).astype(BF16)
        elif cblk < 3:
            qkv_ref[0, :, lo:lo + 512] = p.astype(BF16)
        else:
            s_ref[0] = p


def _in_projection(x, mod3, w_in_b):
    tm = PROJ_ROWS
    return pl.pallas_call(
        _inproj_kernel,
        out_shape=(jax.ShapeDtypeStruct((BATCH, SEQ, 3 * ATTN_WIDTH), BF16),
                   jax.ShapeDtypeStruct((BATCH, SEQ, SSM_WIDTH), F32)),
        grid=(BATCH, SEQ // tm),
        in_specs=[pl.BlockSpec((1, tm, D_MODEL), lambda b, i: (b, i, 0)),
                  pl.BlockSpec((1, 1, D_MODEL), lambda b, i: (b, 0, 0)),
                  pl.BlockSpec((1, 1, D_MODEL), lambda b, i: (b, 0, 1)),
                  pl.BlockSpec((D_MODEL, 4 * 512), lambda b, i: (0, 0))],
        out_specs=(pl.BlockSpec((1, tm, 3 * ATTN_WIDTH), lambda b, i: (b, i, 0)),
                   pl.BlockSpec((1, tm, SSM_WIDTH), lambda b, i: (b, i, 0))),
        compiler_params=pltpu.CompilerParams(vmem_limit_bytes=VMEM_LIMIT_BYTES),
        name="ln_inproj",
    )(x, mod3, mod3, w_in_b)


def _t5_bucket_np(dist):
    max_exact = REL_BUCKETS // 2
    distf = np.maximum(dist, 1).astype(np.float32)
    large = max_exact + (np.log(distf / max_exact) / math.log(REL_MAX_DIST / max_exact)
                         * (REL_BUCKETS - max_exact)).astype(np.int32)
    large = np.minimum(large, REL_BUCKETS - 1)
    return np.where(dist < max_exact, dist, large).astype(np.int32)


def _bucket_tiles():
    qpos = np.arange(ATTN_TQ)[:, None]
    kpos = np.arange(ATTN_TK)[None, :]
    diag = qpos - kpos
    diag_b = np.where(diag >= 0, _t5_bucket_np(np.maximum(diag, 0)), -1)
    sub_b = _t5_bucket_np(diag + ATTN_TK)
    return np.stack([diag_b, sub_b]).astype(np.int32)


def _bias_kernel(tab_ref, bucket_ref, o_ref):
    h = pl.program_id(0)
    for tile in range(2):
        b = bucket_ref[tile]
        acc = jnp.full(b.shape, NEG_INF, F32)
        for n in range(REL_BUCKETS):
            acc = jnp.where(b == n, tab_ref[n, h], acc)
        o_ref[0, tile, 0:ATTN_TQ, :] = acc
        o_ref[0, tile, ATTN_TQ:2 * ATTN_TQ, :] = acc


def _bias_tiles(rel_bias):
    buckets = jnp.asarray(_bucket_tiles())
    return pl.pallas_call(
        _bias_kernel,
        out_shape=jax.ShapeDtypeStruct((ATTN_HEADS, 2, 2 * ATTN_TQ, ATTN_TK), F32),
        grid=(ATTN_HEADS,),
        in_specs=[pl.BlockSpec(memory_space=pltpu.SMEM),
                  pl.BlockSpec((2, ATTN_TQ, ATTN_TK), lambda h: (0, 0, 0))],
        out_specs=pl.BlockSpec((1, 2, 2 * ATTN_TQ, ATTN_TK), lambda h: (h, 0, 0, 0)),
        name="t5_bias_tiles",
    )(rel_bias, buckets)


def _attn_kernel(tab_ref, q_ref, k_ref, v_ref, bias_ref, lq1_ref, lk1_ref, lq2_ref, lk2_ref, g_ref,
                 o_ref, m_ref, l_ref, acc_ref, *, lam_init):
    h = pl.program_id(1)
    qi = pl.program_id(2)
    tq, tk = ATTN_TQ, ATTN_TK

    q = q_ref[0].astype(F32)
    lane = lax.broadcasted_iota(jnp.int32, q.shape, 1)
    qs = jnp.concatenate([jnp.where(lane < HEAD_DIM, q, 0.0),
                          jnp.where(lane >= HEAD_DIM, q, 0.0)], axis=0).astype(BF16)

    def scores(j):
        k = k_ref[0, pl.ds(pl.multiple_of(j * tk, tk), tk), :]
        return lax.dot_general(qs, k, (((1,), (1,)), ((), ())), preferred_element_type=F32)

    def update(j, s, first):
        v = v_ref[0, pl.ds(pl.multiple_of(j * tk, tk), tk), :]
        m_cur = jnp.max(s, axis=1, keepdims=True)
        if first:
            p = jnp.exp(s - m_cur)
            l_ref[...] = jnp.sum(p, axis=1, keepdims=True)
            acc_ref[...] = jnp.dot(p.astype(BF16), v, preferred_element_type=F32)
            m_ref[...] = m_cur
        else:
            m_prev = m_ref[...]
            m_new = jnp.maximum(m_prev, m_cur)
            alpha = jnp.exp(m_prev - m_new)
            p = jnp.exp(s - m_new)
            l_ref[...] = alpha * l_ref[...] + jnp.sum(p, axis=1, keepdims=True)
            acc_ref[...] = alpha * acc_ref[...] + jnp.dot(p.astype(BF16), v, preferred_element_type=F32)
            m_ref[...] = m_new

    update(qi, scores(qi) + bias_ref[0, 0], True)

    @pl.when(qi >= 1)
    def _():
        update(qi - 1, scores(qi - 1) + bias_ref[0, 1], False)

    far = tab_ref[REL_BUCKETS - 1, h]

    def body(j, carry):
        update(j, scores(j) + far, False)
        return carry

    lax.fori_loop(0, jnp.maximum(qi - 1, 0), body, 0)

    lam = (jnp.exp(jnp.sum(lq1_ref[...] * lk1_ref[...], axis=1, keepdims=True))
           - jnp.exp(jnp.sum(lq2_ref[...] * lk2_ref[...], axis=1, keepdims=True)) + lam_init)
    on = acc_ref[...] / l_ref[...]
    o = on[0:tq] - lam * on[tq:2 * tq]
    ms = jnp.mean(o * o, axis=1, keepdims=True)
    o = o * lax.rsqrt(ms + LN_EPS) * g_ref[...] * (1.0 - lam_init)
    o_ref[0] = o.astype(BF16)


def _attention(qkv, bias_tiles, rel_bias, lam_q1, lam_k1, lam_q2, lam_k2, subln_g, lam_init):
    tq = ATTN_TQ
    nh = ATTN_HEADS
    small = lambda b, h, i: (0, 0)
    return pl.pallas_call(
        functools.partial(_attn_kernel, lam_init=lam_init),
        out_shape=jax.ShapeDtypeStruct((BATCH, SEQ, ATTN_WIDTH), BF16),
        grid=(BATCH, nh, SEQ // tq),
        in_specs=[pl.BlockSpec(memory_space=pltpu.SMEM),
                  pl.BlockSpec((1, tq, HEAD_COLS), lambda b, h, i: (b, i, h)),
                  pl.BlockSpec((1, SEQ, HEAD_COLS), lambda b, h, i: (b, 0, nh + h)),
                  pl.BlockSpec((1, SEQ, HEAD_COLS), lambda b, h, i: (b, 0, 2 * nh + h)),
                  pl.BlockSpec((1, 2, 2 * tq, ATTN_TK), lambda b, h, i: (h, 0, 0, 0)),
                  pl.BlockSpec((1, HEAD_DIM), small), pl.BlockSpec((1, HEAD_DIM), small),
                  pl.BlockSpec((1, HEAD_DIM), small), pl.BlockSpec((1, HEAD_DIM), small),
                  pl.BlockSpec((1, HEAD_COLS), small)],
        out_specs=pl.BlockSpec((1, tq, HEAD_COLS), lambda b, h, i: (b, i, h)),
        scratch_shapes=[pltpu.VMEM((2 * tq, 1), F32), pltpu.VMEM((2 * tq, 1), F32),
                        pltpu.VMEM((2 * tq, HEAD_COLS), F32)],
        compiler_params=pltpu.CompilerParams(vmem_limit_bytes=VMEM_LIMIT_BYTES),
        name="diff_attention",
    )(rel_bias, qkv, qkv, qkv, bias_tiles, lam_q1, lam_k1, lam_q2, lam_k2, subln_g)


def _ssm_prep_kernel(are_ref, aim_ref, ldt_ref, ebr_ref, ebi_ref, ecr_ref, eci_ref,
                     eb_ref, ec_ref, abar_ref):
    ar = are_ref[0]
    ai = aim_ref[0]
    dt = jnp.exp(ldt_ref[0])
    mag = jnp.exp(ar * dt)
    ang = ai * dt
    pr = mag * jnp.cos(ang)
    pim = mag * jnp.sin(ang)
    den = ar * ar + ai * ai
    fr = ((pr - 1.0) * ar + pim * ai) / den
    fi = (pim * ar - (pr - 1.0) * ai) / den
    ebr = ebr_ref[0]
    ebi = ebi_ref[0]
    ns = SSM_HALF_STATE
    eb_ref[0, :, 0:ns] = (fr * ebr - fi * ebi).astype(BF16)
    eb_ref[0, :, ns:2 * ns] = (fr * ebi + fi * ebr).astype(BF16)
    ec_ref[0, 0:ns, :] = ecr_ref[0].astype(BF16)
    ec_ref[0, ns:2 * ns, :] = (-eci_ref[0]).astype(BF16)
    qr, qi = pr, pim
    for _ in range(int(math.log2(SSM_CHUNK))):
        qr, qi = qr * qr - qi * qi, 2.0 * qr * qi
    abar_ref[0] = jnp.concatenate([pr, pim, qr, qi, jnp.zeros((4, ns), F32)], axis=0)


def _block_diag_half(w):
    g2 = SSM_GROUPS // 2
    w = w.reshape(2, g2, w.shape[1], w.shape[2])
    eye = jnp.eye(g2, dtype=w.dtype)
    out = w[:, :, :, None, :] * eye[None, :, None, :, None]
    return out.reshape(2, g2 * w.shape[2], g2 * w.shape[3])


def _ssm_prep(a_re, a_im, b_re, b_im, c_re, c_im, log_dt):
    ns = SSM_HALF_STATE
    row = lambda v: v.reshape(2, 1, ns)
    ldt = jnp.broadcast_to(log_dt[:, None], (SSM_GROUPS, SSM_STATE))
    ebr = _block_diag_half(jnp.transpose(b_re, (0, 2, 1)))
    ebi = _block_diag_half(jnp.transpose(b_im, (0, 2, 1)))
    ecr = _block_diag_half(jnp.transpose(c_re, (0, 2, 1)))
    eci = _block_diag_half(jnp.transpose(c_im, (0, 2, 1)))
    vec = pl.BlockSpec((1, 1, ns), lambda j: (j, 0, 0))
    return pl.pallas_call(
        _ssm_prep_kernel,
        out_shape=(jax.ShapeDtypeStruct((2, SSM_HALF, 2 * ns), BF16),
                   jax.ShapeDtypeStruct((2, 2 * ns, SSM_HALF), BF16),
                   jax.ShapeDtypeStruct((2, 8, ns), F32)),
        grid=(2,),
        in_specs=[vec, vec, vec,
                  pl.BlockSpec((1, SSM_HALF, ns), lambda j: (j, 0, 0)),
                  pl.BlockSpec((1, SSM_HALF, ns), lambda j: (j, 0, 0)),
                  pl.BlockSpec((1, ns, SSM_HALF), lambda j: (j, 0, 0)),
                  pl.BlockSpec((1, ns, SSM_HALF), lambda j: (j, 0, 0))],
        out_specs=(pl.BlockSpec((1, SSM_HALF, 2 * ns), lambda j: (j, 0, 0)),
                   pl.BlockSpec((1, 2 * ns, SSM_HALF), lambda j: (j, 0, 0)),
                   pl.BlockSpec((1, 8, ns), lambda j: (j, 0, 0))),
        name="ssm_discretise",
    )(row(a_re), row(a_im), row(ldt), ebr, ebi, ecr, eci)


def _ssm_kernel(s_ref, eb_ref, ec_ref, abar_ref, d_ref, wglu_ref, bglu_ref, y_ref, h_ref, carry_ref):
    nb, nr, nt, ns = BATCH, SSM_ROWS, SSM_CHUNK, SSM_HALF_STATE
    rows = nb * nr

    @pl.when(pl.program_id(0) == 0)
    def _():
        carry_ref[...] = jnp.zeros_like(carry_ref)

    def s_tile(t, lo, width):
        start = pl.multiple_of(t * SSM_WIDTH + lo, LANES)
        return s_ref[:, :, pl.ds(start, width)].reshape(rows, width)

    ntile = ns // LANES

    def advance(j, t):
        bu = jnp.dot(s_tile(t, j * SSM_HALF, SSM_HALF).astype(BF16), eb_ref[j],
                     preferred_element_type=F32)
        tiles = [None] * (2 * ntile)
        for k in range(ntile):
            lo = k * LANES
            ar = abar_ref[j, 0:1, lo:lo + LANES]
            ai = abar_ref[j, 1:2, lo:lo + LANES]
            hr = h_ref[j, k]
            hi = h_ref[j, ntile + k]
            new_r = ar * hr - ai * hi + bu[:, lo:lo + LANES]
            new_i = ar * hi + ai * hr + bu[:, ns + lo:ns + lo + LANES]
            h_ref[j, k] = new_r
            h_ref[j, ntile + k] = new_i
            tiles[k], tiles[ntile + k] = new_r, new_i
        return tiles

    h_ref[...] = jnp.zeros_like(h_ref)

    def local_step(t, carry):
        for j in range(2):
            advance(j, t)
        return carry

    lax.fori_loop(0, nt, local_step, 0)

    for j in range(2):
        for k in range(ntile):
            lo = k * LANES
            qr = abar_ref[j, 2:3, lo:lo + LANES]
            qi = abar_ref[j, 3:4, lo:lo + LANES]
            st_r = carry_ref[j, k]
            st_i = carry_ref[j, ntile + k]
            for c in range(nr):
                sel = pl.ds(c, nb, stride=nr)
                loc_r = h_ref[j, k, sel, :]
                loc_i = h_ref[j, ntile + k, sel, :]
                h_ref[j, k, sel, :] = st_r
                h_ref[j, ntile + k, sel, :] = st_i
                st_r, st_i = qr * st_r - qi * st_i + loc_r, qr * st_i + qi * st_r + loc_i
            carry_ref[j, k] = st_r
            carry_ref[j, ntile + k] = st_i

    def output_step(t, carry):
        ys = []
        for j in range(2):
            hcat = jnp.concatenate(advance(j, t), axis=1).astype(BF16)
            ys.append(jnp.dot(hcat, ec_ref[j], preferred_element_type=F32))
        y = jnp.concatenate(ys, axis=1) + d_ref[...] * s_tile(t, 0, SSM_WIDTH)
        y = jax.nn.gelu(y)
        z = jnp.dot(y.astype(BF16), wglu_ref[...], preferred_element_type=F32) + bglu_ref[...]
        out = (y * jax.nn.sigmoid(z)).astype(BF16)
        start = pl.multiple_of(t * SSM_WIDTH, SSM_WIDTH)
        y_ref[:, :, pl.ds(start, SSM_WIDTH)] = out.reshape(nb, nr, SSM_WIDTH)
        return carry

    lax.fori_loop(0, nt, output_step, 0)


def _ssm(s, eb, ec, abar, d_skip, w_glu_b, b_glu):
    nt, nr, ns = SSM_CHUNK, SSM_ROWS, SSM_HALF_STATE
    nchunks = SEQ // nt
    width = nt * SSM_WIDTH
    s3 = s.reshape(BATCH, nchunks, width)
    const3 = lambda i: (0, 0, 0)
    const2 = lambda i: (0, 0)
    y3 = pl.pallas_call(
        _ssm_kernel,
        out_shape=jax.ShapeDtypeStruct((BATCH, nchunks, width), BF16),
        grid=(nchunks // nr,),
        in_specs=[pl.BlockSpec((BATCH, nr, width), lambda i: (0, i, 0)),
                  pl.BlockSpec((2, SSM_HALF, 2 * ns), const3),
                  pl.BlockSpec((2, 2 * ns, SSM_HALF), const3),
                  pl.BlockSpec((2, 8, ns), const3),
                  pl.BlockSpec((1, SSM_WIDTH), const2),
                  pl.BlockSpec((SSM_WIDTH, SSM_WIDTH), const2),
                  pl.BlockSpec((1, SSM_WIDTH), const2)],
        out_specs=pl.BlockSpec((BATCH, nr, width), lambda i: (0, i, 0)),
        scratch_shapes=[pltpu.VMEM((2, 2 * ns // LANES, BATCH * nr, LANES), F32),
                        pltpu.VMEM((2, 2 * ns // LANES, BATCH, LANES), F32)],
        compiler_params=pltpu.CompilerParams(dimension_semantics=("arbitrary",),
                                             vmem_limit_bytes=VMEM_LIMIT_BYTES),
        name="s5_ssm_glu",
    )(s3, eb, ec, abar, d_skip.reshape(1, SSM_WIDTH), w_glu_b, b_glu.reshape(1, SSM_WIDTH))
    return y3.reshape(BATCH, SEQ, SSM_WIDTH)


def _ffn_kernel(x_ref, o_ref, y_ref, g1_ref, sh2_ref, sc2_ref, g2_ref, wo_ref, ln1g_ref, ln1b_ref,
                wup_ref, cw_ref, wdn_ref, ln2g_ref, ln2b_ref, out_ref,
                u_ref, f_ref, hbuf_ref, carry_ref):
    tm = FFN_ROWS

    @pl.when(pl.program_id(1) == 0)
    def _():
        carry_ref[...] = jnp.zeros_like(carry_ref)

    m = (jnp.dot(o_ref[0], wo_ref[0:ATTN_WIDTH, :], preferred_element_type=F32)
         + jnp.dot(y_ref[0], wo_ref[ATTN_WIDTH:, :], preferred_element_type=F32))
    x1 = _layernorm(DEEPNORM_ALPHA * x_ref[0] + g1_ref[0] * m) * ln1g_ref[...] + ln1b_ref[...]
    u_ref[...] = (_layernorm(x1) * (1.0 + sc2_ref[0]) + sh2_ref[0]).astype(BF16)
    out_ref[0] = x1

    def conv_cols(idx):
        hcur = jnp.dot(u_ref[...], wup_ref[idx], preferred_element_type=F32)
        hbuf_ref[0:SUBLANES, :] = carry_ref[idx]
        hbuf_ref[SUBLANES:SUBLANES + tm, :] = hcur
        carry_ref[idx] = hcur[tm - SUBLANES:tm, :]
        w = cw_ref[idx]
        return (w[2:3] * hcur + w[1:2] * hbuf_ref[SUBLANES - 1:SUBLANES - 1 + tm, :]
                + w[0:1] * hbuf_ref[SUBLANES - 2:SUBLANES - 2 + tm, :] + w[3:4])

    def body(j, carry):
        val = conv_cols(j)
        gate = conv_cols(j + FFN_STEPS)
        a = (gate * jax.nn.sigmoid(gate) * val).astype(BF16)
        part = jnp.dot(a, wdn_ref[j], preferred_element_type=F32)

        @pl.when(j == 0)
        def _():
            f_ref[...] = part

        @pl.when(j > 0)
        def _():
            f_ref[...] += part
        return carry

    lax.fori_loop(0, FFN_STEPS, body, 0)
    x1 = out_ref[0]
    out_ref[0] = (_layernorm(DEEPNORM_ALPHA * x1 + g2_ref[0] * f_ref[...]) * ln2g_ref[...]
                  + ln2b_ref[...])


def _outproj_ffn(x, o, y, mod3, w_out_b, ln1_g, ln1_b, w_up_c, conv_c, w_down_c, ln2_g, ln2_b):
    tm = FFN_ROWS
    row = lambda b, i: (b, i, 0)
    modv = lambda k: pl.BlockSpec((1, 1, D_MODEL), lambda b, i: (b, 0, k))
    c2 = lambda b, i: (0, 0)
    c3 = lambda b, i: (0, 0, 0)
    vec = pl.BlockSpec((1, D_MODEL), c2)
    return pl.pallas_call(
        _ffn_kernel,
        out_shape=jax.ShapeDtypeStruct((BATCH, SEQ, D_MODEL), F32),
        grid=(BATCH, SEQ // tm),
        in_specs=[pl.BlockSpec((1, tm, D_MODEL), row),
                  pl.BlockSpec((1, tm, ATTN_WIDTH), row),
                  pl.BlockSpec((1, tm, SSM_WIDTH), row),
                  modv(2), modv(3), modv(4), modv(5),
                  pl.BlockSpec((D_MODEL, D_MODEL), c2, pipeline_mode=pl.Buffered(1)),
                  vec, vec,
                  pl.BlockSpec((2 * FFN_STEPS, D_MODEL, FFN_COLS), c3, pipeline_mode=pl.Buffered(1)),
                  pl.BlockSpec((2 * FFN_STEPS, SUBLANES, FFN_COLS), c3),
                  pl.BlockSpec((FFN_STEPS, FFN_COLS, D_MODEL), c3, pipeline_mode=pl.Buffered(1)),
                  vec, vec],
        out_specs=pl.BlockSpec((1, tm, D_MODEL), row),
        scratch_shapes=[pltpu.VMEM((tm, D_MODEL), BF16),
                        pltpu.VMEM((tm, D_MODEL), F32),
                        pltpu.VMEM((SUBLANES + tm, FFN_COLS), F32),
                        pltpu.VMEM((2 * FFN_STEPS, SUBLANES, FFN_COLS), F32)],
        compiler_params=pltpu.CompilerParams(dimension_semantics=("arbitrary", "arbitrary"),
                                             vmem_limit_bytes=VMEM_LIMIT_BYTES),
        name="outproj_ffn",
    )(x, o, y, mod3, mod3, mod3, mod3, w_out_b, ln1_g, ln1_b, w_up_c, conv_c, w_down_c, ln2_g, ln2_b)


def kernel(x, c, w_ada, b_ada, w_in, lam_q1, lam_k1, lam_q2, lam_k2, subln_g, ssm_a_re, ssm_a_im,
           ssm_b_re, ssm_b_im, ssm_c_re, ssm_c_im, ssm_d, ssm_log_dt, w_glu, b_glu, w_out, ln1_g,
           ln1_b, w_up, conv_w, conv_b, w_down, ln2_g, ln2_b, rel_bias):
    bias_tiles = _bias_tiles(rel_bias)
    for layer in range(DEPTH):
        lam_init = 0.8 - 0.6 * math.exp(-0.3 * layer)
        mod3 = _modulation(c, w_ada[layer], b_ada[layer]).reshape(BATCH, 1, 6 * D_MODEL)
        qkv, s = _in_projection(x, mod3, w_in[layer].astype(BF16))
        o = _attention(qkv, bias_tiles, rel_bias, lam_q1[layer][None], lam_k1[layer][None],
                       lam_q2[layer][None], lam_k2[layer][None], subln_g[layer][None], lam_init)
        eb, ec, abar = _ssm_prep(ssm_a_re[layer], ssm_a_im[layer], ssm_b_re[layer], ssm_b_im[layer],
                                 ssm_c_re[layer], ssm_c_im[layer], ssm_log_dt[layer])
        y = _ssm(s, eb, ec, abar, ssm_d[layer], w_glu[layer].astype(BF16), b_glu[layer])
        w_up_c = jnp.transpose(w_up[layer].astype(BF16).reshape(D_MODEL, 2 * FFN_STEPS, FFN_COLS),
                               (1, 0, 2))
        conv_c = jnp.concatenate([conv_w[layer], conv_b[layer][None],
                                  jnp.zeros((SUBLANES - 4, 2 * D_FF), F32)], axis=0)
        conv_c = jnp.transpose(conv_c.reshape(SUBLANES, 2 * FFN_STEPS, FFN_COLS), (1, 0, 2))
        w_down_c = w_down[layer].astype(BF16).reshape(FFN_STEPS, FFN_COLS, D_MODEL)
        x = _outproj_ffn(x, o, y, mod3, w_out[layer].astype(BF16), ln1_g[layer][None],
                         ln1_b[layer][None], w_up_c, conv_c, w_down_c, ln2_g[layer][None],
                         ln2_b[layer][None])
    return x
```

```python
import functools
import math

import numpy as np
import jax
import jax.numpy as jnp
from jax import lax
from jax.experimental import pallas as pl
from jax.experimental.pallas import tpu as pltpu

F32 = jnp.float32
BF16 = jnp.bfloat16

D_MODEL = 1024
BATCH = 16
SEQ = 4096
ATTN_WIDTH = 512
SSM_WIDTH = 512
ATTN_HEADS = 4
HEAD_DIM = 64
HEAD_COLS = 2 * HEAD_DIM
SSM_GROUP = 16
SSM_GROUPS = 32
SSM_STATE = 64
D_FF = 2816
REL_BUCKETS = 32
REL_MAX_DIST = 128
LN_EPS = 1e-5
NEG_INF = -1e30
DEPTH = 1
DEEPNORM_ALPHA = (2 * DEPTH) ** 0.25

LANES = 128
SUBLANES = 8
MXU_DIM = 256
VMEM_LIMIT_BYTES = 56 * 1024 * 1024

PROJ_ROWS = 512
ATTN_TQ = 512
ATTN_TK = 512
LOG2E = math.log2(math.e)
SSM_CHUNK = 16
SSM_ROWS = 16
SSM_HALF = SSM_WIDTH // 2
SSM_HALF_STATE = (SSM_GROUPS // 2) * SSM_STATE
FFN_ROWS = 512
FFN_COLS = 256
FFN_STEPS = D_FF // FFN_COLS


def _layernorm(x):
    mu = jnp.mean(x, axis=-1, keepdims=True)
    xc = x - mu
    var = jnp.mean(xc * xc, axis=-1, keepdims=True)
    return xc * lax.rsqrt(var + LN_EPS)


def _mod_kernel(c_ref, w_ref, b_ref, o_ref):
    c = c_ref[...]
    act = c * jax.nn.sigmoid(c)
    o_ref[...] = jnp.dot(act, w_ref[...], precision=lax.Precision.HIGHEST,
                         preferred_element_type=F32) + b_ref[...]


def _modulation(c, w_ada, b_ada):
    n = w_ada.shape[1]
    blk = D_MODEL
    return pl.pallas_call(
        _mod_kernel,
        out_shape=jax.ShapeDtypeStruct((BATCH, n), F32),
        grid=(n // blk,),
        in_specs=[pl.BlockSpec((BATCH, D_MODEL), lambda i: (0, 0)),
                  pl.BlockSpec((D_MODEL, blk), lambda i: (0, i)),
                  pl.BlockSpec((1, blk), lambda i: (0, i))],
        out_specs=pl.BlockSpec((BATCH, blk), lambda i: (0, i)),
        name="adaln_mod",
    )(c, w_ada, b_ada.reshape(1, n))


def _inproj_kernel(x_ref, sh_ref, sc_ref, wqt_ref, wk_ref, wvt_ref, ws_ref,
                   qt_ref, k_ref, vt_ref, s_ref):
    u = _layernorm(x_ref[0]) * (1.0 + sc_ref[0]) + sh_ref[0]
    ub = u.astype(BF16)
    nt = (((1,), (1,)), ((), ()))
    qt = lax.dot_general(wqt_ref[...], ub, nt, preferred_element_type=F32)
    qt_ref[0] = (qt * (HEAD_DIM ** -0.5 * LOG2E)).astype(BF16)
    k_ref[0] = jnp.dot(ub, wk_ref[...], preferred_element_type=F32).astype(BF16)
    vt_ref[0] = lax.dot_general(wvt_ref[...], ub, nt, preferred_element_type=F32).astype(BF16)
    s_ref[0] = jnp.dot(ub, ws_ref[...], preferred_element_type=F32)


def _in_projection(x, mod3, w_in_b):
    tm = PROJ_ROWS
    aw = ATTN_WIDTH
    wqt = jnp.transpose(w_in_b[:, 0:aw])
    wk = w_in_b[:, aw:2 * aw]
    wvt = jnp.transpose(w_in_b[:, 2 * aw:3 * aw])
    ws = w_in_b[:, 3 * aw:]
    const = lambda b, i: (0, 0)
    tposed = pl.BlockSpec((1, aw, tm), lambda b, i: (b, 0, i))
    rowblk = lambda w: pl.BlockSpec((1, tm, w), lambda b, i: (b, i, 0))
    return pl.pallas_call(
        _inproj_kernel,
        out_shape=(jax.ShapeDtypeStruct((BATCH, aw, SEQ), BF16),
                   jax.ShapeDtypeStruct((BATCH, SEQ, aw), BF16),
                   jax.ShapeDtypeStruct((BATCH, aw, SEQ), BF16),
                   jax.ShapeDtypeStruct((BATCH, SEQ, SSM_WIDTH), F32)),
        grid=(BATCH, SEQ // tm),
        in_specs=[rowblk(D_MODEL),
                  pl.BlockSpec((1, 1, D_MODEL), lambda b, i: (b, 0, 0)),
                  pl.BlockSpec((1, 1, D_MODEL), lambda b, i: (b, 0, 1)),
                  pl.BlockSpec((aw, D_MODEL), const), pl.BlockSpec((D_MODEL, aw), const),
                  pl.BlockSpec((aw, D_MODEL), const), pl.BlockSpec((D_MODEL, SSM_WIDTH), const)],
        out_specs=(tposed, rowblk(aw), tposed, rowblk(SSM_WIDTH)),
        compiler_params=pltpu.CompilerParams(vmem_limit_bytes=VMEM_LIMIT_BYTES),
        name="ln_inproj",
    )(x, mod3, mod3, wqt, wk, wvt, ws)


def _t5_bucket_np(dist):
    max_exact = REL_BUCKETS // 2
    distf = np.maximum(dist, 1).astype(np.float32)
    large = max_exact + (np.log(distf / max_exact) / math.log(REL_MAX_DIST / max_exact)
                         * (REL_BUCKETS - max_exact)).astype(np.int32)
    large = np.minimum(large, REL_BUCKETS - 1)
    return np.where(dist < max_exact, dist, large).astype(np.int32)


def _bucket_tiles():
    kpos = np.arange(ATTN_TK)[:, None]
    qpos = np.arange(ATTN_TQ)[None, :]
    diag = qpos - kpos
    diag_b = np.where(diag >= 0, _t5_bucket_np(np.maximum(diag, 0)), -1)
    sub_b = _t5_bucket_np(diag + ATTN_TK)
    return np.stack([diag_b, sub_b]).astype(np.int32)


def _bias_kernel(tab_ref, bucket_ref, o_ref):
    h = pl.program_id(0)
    far = tab_ref[REL_BUCKETS - 1, h]
    for tile in range(2):
        b = bucket_ref[tile]
        acc = jnp.full(b.shape, NEG_INF, F32)
        for n in range(REL_BUCKETS):
            acc = jnp.where(b == n, (tab_ref[n, h] - far) * LOG2E, acc)
        o_ref[0, tile, :, 0:ATTN_TQ] = acc
        o_ref[0, tile, :, ATTN_TQ:2 * ATTN_TQ] = acc


def _bias_tiles(rel_bias):
    buckets = jnp.asarray(_bucket_tiles())
    return pl.pallas_call(
        _bias_kernel,
        out_shape=jax.ShapeDtypeStruct((ATTN_HEADS, 2, ATTN_TK, 2 * ATTN_TQ), F32),
        grid=(ATTN_HEADS,),
        in_specs=[pl.BlockSpec(memory_space=pltpu.SMEM),
                  pl.BlockSpec((2, ATTN_TK, ATTN_TQ), lambda h: (0, 0, 0))],
        out_specs=pl.BlockSpec((1, 2, ATTN_TK, 2 * ATTN_TQ), lambda h: (h, 0, 0, 0)),
        name="t5_bias_tiles",
    )(rel_bias, buckets)


def _attn_kernel(qt_ref, k_ref, vt_ref, bias_ref, lq1_ref, lk1_ref, lq2_ref, lk2_ref, g_ref,
                 o_ref, m_ref, l_ref, acc_ref, *, lam_init):
    qi = pl.program_id(2)
    tq, tk = ATTN_TQ, ATTN_TK

    qt = qt_ref[0].astype(F32)
    row = lax.broadcasted_iota(jnp.int32, qt.shape, 0)
    qs = jnp.concatenate([jnp.where(row < HEAD_DIM, qt, 0.0),
                          jnp.where(row >= HEAD_DIM, qt, 0.0)], axis=1).astype(BF16)

    def scores(j):
        k = k_ref[0, pl.ds(pl.multiple_of(j * tk, tk), tk), :]
        return jnp.dot(k, qs, preferred_element_type=F32)

    def update(j, s, first):
        vt = vt_ref[0, :, pl.ds(pl.multiple_of(j * tk, tk), tk)]
        m_cur = jnp.max(s, axis=0, keepdims=True)
        if first:
            p = jnp.exp2(s - m_cur)
            l_ref[...] = jnp.sum(p, axis=0, keepdims=True)
            acc_ref[...] = jnp.dot(vt, p.astype(BF16), preferred_element_type=F32)
            m_ref[...] = m_cur
        else:
            m_prev = m_ref[...]
            m_new = jnp.maximum(m_prev, m_cur)
            alpha = jnp.exp2(m_prev - m_new)
            p = jnp.exp2(s - m_new)
            l_ref[...] = alpha * l_ref[...] + jnp.sum(p, axis=0, keepdims=True)
            acc_ref[...] = alpha * acc_ref[...] + jnp.dot(vt, p.astype(BF16),
                                                          preferred_element_type=F32)
            m_ref[...] = m_new

    update(qi, scores(qi) + bias_ref[0, 0], True)

    @pl.when(qi >= 1)
    def _():
        update(qi - 1, scores(qi - 1) + bias_ref[0, 1], False)

    def body(j, carry):
        update(j, scores(j), False)
        return carry

    lax.fori_loop(0, jnp.maximum(qi - 1, 0), body, 0)

    lam = (jnp.exp(jnp.sum(lq1_ref[...] * lk1_ref[...], axis=1, keepdims=True))
           - jnp.exp(jnp.sum(lq2_ref[...] * lk2_ref[...], axis=1, keepdims=True)) + lam_init)
    on = acc_ref[...] / l_ref[...]
    ot = on[:, 0:tq] - lam * on[:, tq:2 * tq]
    ms = jnp.mean(ot * ot, axis=0, keepdims=True)
    ot = ot * lax.rsqrt(ms + LN_EPS) * g_ref[...] * (1.0 - lam_init)
    o_ref[0] = jnp.transpose(ot).astype(BF16)


def _attention(qt, k, vt, bias_tiles, lam_q1, lam_k1, lam_q2, lam_k2, subln_g, lam_init):
    tq = ATTN_TQ
    small = lambda b, h, i: (0, 0)
    return pl.pallas_call(
        functools.partial(_attn_kernel, lam_init=lam_init),
        out_shape=jax.ShapeDtypeStruct((BATCH, SEQ, ATTN_WIDTH), BF16),
        grid=(BATCH, ATTN_HEADS, SEQ // tq),
        in_specs=[pl.BlockSpec((1, HEAD_COLS, tq), lambda b, h, i: (b, h, i)),
                  pl.BlockSpec((1, SEQ, HEAD_COLS), lambda b, h, i: (b, 0, h)),
                  pl.BlockSpec((1, HEAD_COLS, SEQ), lambda b, h, i: (b, h, 0)),
                  pl.BlockSpec((1, 2, ATTN_TK, 2 * tq), lambda b, h, i: (h, 0, 0, 0)),
                  pl.BlockSpec((1, HEAD_DIM), small), pl.BlockSpec((1, HEAD_DIM), small),
                  pl.BlockSpec((1, HEAD_DIM), small), pl.BlockSpec((1, HEAD_DIM), small),
                  pl.BlockSpec((HEAD_COLS, 1), small)],
        out_specs=pl.BlockSpec((1, tq, HEAD_COLS), lambda b, h, i: (b, i, h)),
        scratch_shapes=[pltpu.VMEM((1, 2 * tq), F32), pltpu.VMEM((1, 2 * tq), F32),
                        pltpu.VMEM((HEAD_COLS, 2 * tq), F32)],
        compiler_params=pltpu.CompilerParams(vmem_limit_bytes=VMEM_LIMIT_BYTES),
        name="diff_attention",
    )(qt, k, vt, bias_tiles, lam_q1, lam_k1, lam_q2, lam_k2, subln_g)


def _ssm_prep_kernel(are_ref, aim_ref, ldt_ref, ebr_ref, ebi_ref, ecr_ref, eci_ref,
                     eb_ref, ec_ref, abar_ref):
    ar = are_ref[0]
    ai = aim_ref[0]
    dt = jnp.exp(ldt_ref[0])
    mag = jnp.exp(ar * dt)
    ang = ai * dt
    pr = mag * jnp.cos(ang)
    pim = mag * jnp.sin(ang)
    den = ar * ar + ai * ai
    fr = ((pr - 1.0) * ar + pim * ai) / den
    fi = (pim * ar - (pr - 1.0) * ai) / den
    ebr = ebr_ref[0]
    ebi = ebi_ref[0]
    ns = SSM_HALF_STATE
    eb_ref[0, :, 0:ns] = (fr * ebr - fi * ebi).astype(BF16)
    eb_ref[0, :, ns:2 * ns] = (fr * ebi + fi * ebr).astype(BF16)
    ec_ref[0, 0:ns, :] = ecr_ref[0].astype(BF16)
    ec_ref[0, ns:2 * ns, :] = (-eci_ref[0]).astype(BF16)
    qr, qi = pr, pim
    for _ in range(int(math.log2(SSM_CHUNK))):
        qr, qi = qr * qr - qi * qi, 2.0 * qr * qi
    abar_ref[0] = jnp.concatenate([pr, pim, qr, qi, jnp.zeros((4, ns), F32)], axis=0)


def _block_diag_half(w):
    g2 = SSM_GROUPS // 2
    w = w.reshape(2, g2, w.shape[1], w.shape[2])
    eye = jnp.eye(g2, dtype=w.dtype)
    out = w[:, :, :, None, :] * eye[None, :, None, :, None]
    return out.reshape(2, g2 * w.shape[2], g2 * w.shape[3])


def _ssm_prep(a_re, a_im, b_re, b_im, c_re, c_im, log_dt):
    ns = SSM_HALF_STATE
    row = lambda v: v.reshape(2, 1, ns)
    ldt = jnp.broadcast_to(log_dt[:, None], (SSM_GROUPS, SSM_STATE))
    ebr = _block_diag_half(jnp.transpose(b_re, (0, 2, 1)))
    ebi = _block_diag_half(jnp.transpose(b_im, (0, 2, 1)))
    ecr = _block_diag_half(jnp.transpose(c_re, (0, 2, 1)))
    eci = _block_diag_half(jnp.transpose(c_im, (0, 2, 1)))
    vec = pl.BlockSpec((1, 1, ns), lambda j: (j, 0, 0))
    return pl.pallas_call(
        _ssm_prep_kernel,
        out_shape=(jax.ShapeDtypeStruct((2, SSM_HALF, 2 * ns), BF16),
                   jax.ShapeDtypeStruct((2, 2 * ns, SSM_HALF), BF16),
                   jax.ShapeDtypeStruct((2, 8, ns), F32)),
        grid=(2,),
        in_specs=[vec, vec, vec,
                  pl.BlockSpec((1, SSM_HALF, ns), lambda j: (j, 0, 0)),
                  pl.BlockSpec((1, SSM_HALF, ns), lambda j: (j, 0, 0)),
                  pl.BlockSpec((1, ns, SSM_HALF), lambda j: (j, 0, 0)),
                  pl.BlockSpec((1, ns, SSM_HALF), lambda j: (j, 0, 0))],
        out_specs=(pl.BlockSpec((1, SSM_HALF, 2 * ns), lambda j: (j, 0, 0)),
                   pl.BlockSpec((1, 2 * ns, SSM_HALF), lambda j: (j, 0, 0)),
                   pl.BlockSpec((1, 8, ns), lambda j: (j, 0, 0))),
        name="ssm_discretise",
    )(row(a_re), row(a_im), row(ldt), ebr, ebi, ecr, eci)


def _ssm_kernel(s_ref, eb_ref, ec_ref, abar_ref, d_ref, wglu_ref, bglu_ref, y_ref, h_ref, carry_ref):
    nb, nr, nt, ns = BATCH, SSM_ROWS, SSM_CHUNK, SSM_HALF_STATE
    rows = nb * nr

    @pl.when(pl.program_id(0) == 0)
    def _():
        carry_ref[...] = jnp.zeros_like(carry_ref)

    def s_tile(t, lo, width):
        start = pl.multiple_of(t * SSM_WIDTH + lo, LANES)
        return s_ref[:, :, pl.ds(start, width)].reshape(rows, width)

    ntile = ns // LANES

    def advance(j, t):
        bu = jnp.dot(s_tile(t, j * SSM_HALF, SSM_HALF).astype(BF16), eb_ref[j],
                     preferred_element_type=F32)
        tiles = [None] * (2 * ntile)
        for k in range(ntile):
            lo = k * LANES
            ar = abar_ref[j, 0:1, lo:lo + LANES]
            ai = abar_ref[j, 1:2, lo:lo + LANES]
            hr = h_ref[j, k]
            hi = h_ref[j, ntile + k]
            new_r = ar * hr - ai * hi + bu[:, lo:lo + LANES]
            new_i = ar * hi + ai * hr + bu[:, ns + lo:ns + lo + LANES]
            h_ref[j, k] = new_r
            h_ref[j, ntile + k] = new_i
            tiles[k], tiles[ntile + k] = new_r, new_i
        return tiles

    h_ref[...] = jnp.zeros_like(h_ref)

    def local_step(t, carry):
        for j in range(2):
            advance(j, t)
        return carry

    lax.fori_loop(0, nt, local_step, 0)

    for j in range(2):
        for k in range(ntile):
            lo = k * LANES
            qr = abar_ref[j, 2:3, lo:lo + LANES]
            qi = abar_ref[j, 3:4, lo:lo + LANES]
            st_r = carry_ref[j, k]
            st_i = carry_ref[j, ntile + k]
            for c in range(nr):
                sel = pl.ds(c, nb, stride=nr)
                loc_r = h_ref[j, k, sel, :]
                loc_i = h_ref[j, ntile + k, sel, :]
                h_ref[j, k, sel, :] = st_r
                h_ref[j, ntile + k, sel, :] = st_i
                st_r, st_i = qr * st_r - qi * st_i + loc_r, qr * st_i + qi * st_r + loc_i
            carry_ref[j, k] = st_r
            carry_ref[j, ntile + k] = st_i

    def output_step(t, carry):
        ys = []
        for j in range(2):
            hcat = jnp.concatenate(advance(j, t), axis=1).astype(BF16)
            ys.append(jnp.dot(hcat, ec_ref[j], preferred_element_type=F32))
        y = jnp.concatenate(ys, axis=1) + d_ref[...] * s_tile(t, 0, SSM_WIDTH)
        y = jax.nn.gelu(y)
        z = jnp.dot(y.astype(BF16), wglu_ref[...], preferred_element_type=F32) + bglu_ref[...]
        out = (y * jax.nn.sigmoid(z)).astype(BF16)
        start = pl.multiple_of(t * SSM_WIDTH, SSM_WIDTH)
        y_ref[:, :, pl.ds(start, SSM_WIDTH)] = out.reshape(nb, nr, SSM_WIDTH)
        return carry

    lax.fori_loop(0, nt, output_step, 0)


def _ssm(s, eb, ec, abar, d_skip, w_glu_b, b_glu):
    nt, nr, ns = SSM_CHUNK, SSM_ROWS, SSM_HALF_STATE
    nchunks = SEQ // nt
    width = nt * SSM_WIDTH
    s3 = s.reshape(BATCH, nchunks, width)
    const3 = lambda i: (0, 0, 0)
    const2 = lambda i: (0, 0)
    y3 = pl.pallas_call(
        _ssm_kernel,
        out_shape=jax.ShapeDtypeStruct((BATCH, nchunks, width), BF16),
        grid=(nchunks // nr,),
        in_specs=[pl.BlockSpec((BATCH, nr, width), lambda i: (0, i, 0)),
                  pl.BlockSpec((2, SSM_HALF, 2 * ns), const3),
                  pl.BlockSpec((2, 2 * ns, SSM_HALF), const3),
                  pl.BlockSpec((2, 8, ns), const3),
                  pl.BlockSpec((1, SSM_WIDTH), const2),
                  pl.BlockSpec((SSM_WIDTH, SSM_WIDTH), const2),
                  pl.BlockSpec((1, SSM_WIDTH), const2)],
        out_specs=pl.BlockSpec((BATCH, nr, width), lambda i: (0, i, 0)),
        scratch_shapes=[pltpu.VMEM((2, 2 * ns // LANES, BATCH * nr, LANES), F32),
                        pltpu.VMEM((2, 2 * ns // LANES, BATCH, LANES), F32)],
        compiler_params=pltpu.CompilerParams(dimension_semantics=("arbitrary",),
                                             vmem_limit_bytes=VMEM_LIMIT_BYTES),
        name="s5_ssm_glu",
    )(s3, eb, ec, abar, d_skip.reshape(1, SSM_WIDTH), w_glu_b, b_glu.reshape(1, SSM_WIDTH))
    return y3.reshape(BATCH, SEQ, SSM_WIDTH)


def _ffn_kernel(x_ref, o_ref, y_ref, g1_ref, sh2_ref, sc2_ref, g2_ref, wo_ref, ln1g_ref, ln1b_ref,
                wup_ref, cw_ref, wdn_ref, ln2g_ref, ln2b_ref, out_ref,
                u_ref, f_ref, hbuf_ref, carry_ref):
    tm = FFN_ROWS

    @pl.when(pl.program_id(1) == 0)
    def _():
        carry_ref[...] = jnp.zeros_like(carry_ref)

    m = (jnp.dot(o_ref[0], wo_ref[0:ATTN_WIDTH, :], preferred_element_type=F32)
         + jnp.dot(y_ref[0], wo_ref[ATTN_WIDTH:, :], preferred_element_type=F32))
    x1 = _layernorm(DEEPNORM_ALPHA * x_ref[0] + g1_ref[0] * m) * ln1g_ref[...] + ln1b_ref[...]
    u_ref[...] = (_layernorm(x1) * (1.0 + sc2_ref[0]) + sh2_ref[0]).astype(BF16)
    out_ref[0] = x1

    def conv_cols(idx):
        hcur = jnp.dot(u_ref[...], wup_ref[idx], preferred_element_type=F32)
        hbuf_ref[0:SUBLANES, :] = carry_ref[idx]
        hbuf_ref[SUBLANES:SUBLANES + tm, :] = hcur
        carry_ref[idx] = hcur[tm - SUBLANES:tm, :]
        w = cw_ref[idx]
        return (w[2:3] * hcur + w[1:2] * hbuf_ref[SUBLANES - 1:SUBLANES - 1 + tm, :]
                + w[0:1] * hbuf_ref[SUBLANES - 2:SUBLANES - 2 + tm, :] + w[3:4])

    def body(j, carry):
        val = conv_cols(j)
        gate = conv_cols(j + FFN_STEPS)
        a = (gate * jax.nn.sigmoid(gate) * val).astype(BF16)
        part = jnp.dot(a, wdn_ref[j], preferred_element_type=F32)

        @pl.when(j == 0)
        def _():
            f_ref[...] = part

        @pl.when(j > 0)
        def _():
            f_ref[...] += part
        return carry

    lax.fori_loop(0, FFN_STEPS, body, 0)
    x1 = out_ref[0]
    out_ref[0] = (_layernorm(DEEPNORM_ALPHA * x1 + g2_ref[0] * f_ref[...]) * ln2g_ref[...]
                  + ln2b_ref[...])


def _outproj_ffn(x, o, y, mod3, w_out_b, ln1_g, ln1_b, w_up_c, conv_c, w_down_c, ln2_g, ln2_b):
    tm = FFN_ROWS
    row = lambda b, i: (b, i, 0)
    modv = lambda k: pl.BlockSpec((1, 1, D_MODEL), lambda b, i: (b, 0, k))
    c2 = lambda b, i: (0, 0)
    c3 = lambda b, i: (0, 0, 0)
    vec = pl.BlockSpec((1, D_MODEL), c2)
    return pl.pallas_call(
        _ffn_kernel,
        out_shape=jax.ShapeDtypeStruct((BATCH, SEQ, D_MODEL), F32),
        grid=(BATCH, SEQ // tm),
        in_specs=[pl.BlockSpec((1, tm, D_MODEL), row),
                  pl.BlockSpec((1, tm, ATTN_WIDTH), row),
                  pl.BlockSpec((1, tm, SSM_WIDTH), row),
                  modv(2), modv(3), modv(4), modv(5),
                  pl.BlockSpec((D_MODEL, D_MODEL), c2, pipeline_mode=pl.Buffered(1)),
                  vec, vec,
                  pl.BlockSpec((2 * FFN_STEPS, D_MODEL, FFN_COLS), c3, pipeline_mode=pl.Buffered(1)),
                  pl.BlockSpec((2 * FFN_STEPS, SUBLANES, FFN_COLS), c3),
                  pl.BlockSpec((FFN_STEPS, FFN_COLS, D_MODEL), c3, pipeline_mode=pl.Buffered(1)),
                  vec, vec],
        out_specs=pl.BlockSpec((1, tm, D_MODEL), row),
        scratch_shapes=[pltpu.VMEM((tm, D_MODEL), BF16),
                        pltpu.VMEM((tm, D_MODEL), F32),
                        pltpu.VMEM((SUBLANES + tm, FFN_COLS), F32),
                        pltpu.VMEM((2 * FFN_STEPS, SUBLANES, FFN_COLS), F32)],
        compiler_params=pltpu.CompilerParams(dimension_semantics=("arbitrary", "arbitrary"),
                                             vmem_limit_bytes=VMEM_LIMIT_BYTES),
        name="outproj_ffn",
    )(x, o, y, mod3, mod3, mod3, mod3, w_out_b, ln1_g, ln1_b, w_up_c, conv_c, w_down_c, ln2_g, ln2_b)


def kernel(x, c, w_ada, b_ada, w_in, lam_q1, lam_k1, lam_q2, lam_k2, subln_g, ssm_a_re, ssm_a_im,
           ssm_b_re, ssm_b_im, ssm_c_re, ssm_c_im, ssm_d, ssm_log_dt, w_glu, b_glu, w_out, ln1_g,
           ln1_b, w_up, conv_w, conv_b, w_down, ln2_g, ln2_b, rel_bias):
    bias_tiles = _bias_tiles(rel_bias)
    for layer in range(DEPTH):
        lam_init = 0.8 - 0.6 * math.exp(-0.3 * layer)
        mod3 = _modulation(c, w_ada[layer], b_ada[layer]).reshape(BATCH, 1, 6 * D_MODEL)
        qt, k, vt, s = _in_projection(x, mod3, w_in[layer].astype(BF16))
        o = _attention(qt, k, vt, bias_tiles, lam_q1[layer][None], lam_k1[layer][None],
                       lam_q2[layer][None], lam_k2[layer][None], subln_g[layer][:, None], lam_init)
        eb, ec, abar = _ssm_prep(ssm_a_re[layer], ssm_a_im[layer], ssm_b_re[layer], ssm_b_im[layer],
                                 ssm_c_re[layer], ssm_c_im[layer], ssm_log_dt[layer])
        y = _ssm(s, eb, ec, abar, ssm_d[layer], w_glu[layer].astype(BF16), b_glu[layer])
        w_up_c = jnp.transpose(w_up[layer].astype(BF16).reshape(D_MODEL, 2 * FFN_STEPS, FFN_COLS),
                               (1, 0, 2))
        conv_c = jnp.concatenate([conv_w[layer], conv_b[layer][None],
                                  jnp.zeros((SUBLANES - 4, 2 * D_FF), F32)], axis=0)
        conv_c = jnp.transpose(conv_c.reshape(SUBLANES, 2 * FFN_STEPS, FFN_COLS), (1, 0, 2))
        w_down_c = w_down[layer].astype(BF16).reshape(FFN_STEPS, FFN_COLS, D_MODEL)
        x = _outproj_ffn(x, o, y, mod3, w_out[layer].astype(BF16), ln1_g[layer][None],
                         ln1_b[layer][None], w_up_c, conv_c, w_down_c, ln2_g[layer][None],
                         ln2_b[layer][None])
    return x
```

```python
import functools
import math

import numpy as np
import jax
import jax.numpy as jnp
from jax import lax
from jax.experimental import pallas as pl
from jax.experimental.pallas import tpu as pltpu

F32 = jnp.float32
BF16 = jnp.bfloat16

D_MODEL = 1024
BATCH = 16
SEQ = 4096
ATTN_WIDTH = 512
SSM_WIDTH = 512
ATTN_HEADS = 4
HEAD_DIM = 64
HEAD_COLS = 2 * HEAD_DIM
SSM_GROUP = 16
SSM_GROUPS = 32
SSM_STATE = 64
D_FF = 2816
REL_BUCKETS = 32
REL_MAX_DIST = 128
LN_EPS = 1e-5
NEG_INF = -1e30
DEPTH = 1
DEEPNORM_ALPHA = (2 * DEPTH) ** 0.25

LANES = 128
SUBLANES = 8
MXU_DIM = 256
VMEM_LIMIT_BYTES = 56 * 1024 * 1024

PROJ_ROWS = 512
ATTN_TQ = 512
ATTN_TK = 512
ATTN_COLS = MXU_DIM
LOG2E = math.log2(math.e)
SSM_CHUNK = 16
SSM_ROWS = 16
SSM_HALF = SSM_WIDTH // 2
SSM_HALF_STATE = (SSM_GROUPS // 2) * SSM_STATE
FFN_ROWS = 512
FFN_COLS = 256
FFN_STEPS = D_FF // FFN_COLS


def _layernorm(x):
    mu = jnp.mean(x, axis=-1, keepdims=True)
    xc = x - mu
    var = jnp.mean(xc * xc, axis=-1, keepdims=True)
    return xc * lax.rsqrt(var + LN_EPS)


def _mod_kernel(c_ref, w_ref, b_ref, o_ref):
    c = c_ref[...]
    act = c * jax.nn.sigmoid(c)
    o_ref[...] = jnp.dot(act, w_ref[...], precision=lax.Precision.HIGHEST,
                         preferred_element_type=F32) + b_ref[...]


def _modulation(c, w_ada, b_ada):
    n = w_ada.shape[1]
    blk = D_MODEL
    return pl.pallas_call(
        _mod_kernel,
        out_shape=jax.ShapeDtypeStruct((BATCH, n), F32),
        grid=(n // blk,),
        in_specs=[pl.BlockSpec((BATCH, D_MODEL), lambda i: (0, 0)),
                  pl.BlockSpec((D_MODEL, blk), lambda i: (0, i)),
                  pl.BlockSpec((1, blk), lambda i: (0, i))],
        out_specs=pl.BlockSpec((BATCH, blk), lambda i: (0, i)),
        name="adaln_mod",
    )(c, w_ada, b_ada.reshape(1, n))


def _inproj_kernel(x_ref, sh_ref, sc_ref, wqt_ref, wk_ref, wvt_ref, ws_ref,
                   qt_ref, k_ref, vt_ref, s_ref):
    u = _layernorm(x_ref[0]) * (1.0 + sc_ref[0]) + sh_ref[0]
    ub = u.astype(BF16)
    nt = (((1,), (1,)), ((), ()))
    qt = lax.dot_general(wqt_ref[...], ub, nt, preferred_element_type=F32)
    qt_ref[0] = (qt * (HEAD_DIM ** -0.5 * LOG2E)).astype(BF16)
    k_ref[0] = jnp.dot(ub, wk_ref[...], preferred_element_type=F32).astype(BF16)
    vt_ref[0] = lax.dot_general(wvt_ref[...], ub, nt, preferred_element_type=F32).astype(BF16)
    s_ref[0] = jnp.dot(ub, ws_ref[...], preferred_element_type=F32)


def _in_projection(x, mod3, w_in_b):
    tm = PROJ_ROWS
    aw = ATTN_WIDTH
    wqt = jnp.transpose(w_in_b[:, 0:aw])
    wk = w_in_b[:, aw:2 * aw]
    wvt = jnp.transpose(w_in_b[:, 2 * aw:3 * aw])
    ws = w_in_b[:, 3 * aw:]
    const = lambda b, i: (0, 0)
    tposed = pl.BlockSpec((1, aw, tm), lambda b, i: (b, 0, i))
    rowblk = lambda w: pl.BlockSpec((1, tm, w), lambda b, i: (b, i, 0))
    return pl.pallas_call(
        _inproj_kernel,
        out_shape=(jax.ShapeDtypeStruct((BATCH, aw, SEQ), BF16),
                   jax.ShapeDtypeStruct((BATCH, SEQ, aw), BF16),
                   jax.ShapeDtypeStruct((BATCH, aw, SEQ), BF16),
                   jax.ShapeDtypeStruct((BATCH, SEQ, SSM_WIDTH), F32)),
        grid=(BATCH, SEQ // tm),
        in_specs=[rowblk(D_MODEL),
                  pl.BlockSpec((1, 1, D_MODEL), lambda b, i: (b, 0, 0)),
                  pl.BlockSpec((1, 1, D_MODEL), lambda b, i: (b, 0, 1)),
                  pl.BlockSpec((aw, D_MODEL), const), pl.BlockSpec((D_MODEL, aw), const),
                  pl.BlockSpec((aw, D_MODEL), const), pl.BlockSpec((D_MODEL, SSM_WIDTH), const)],
        out_specs=(tposed, rowblk(aw), tposed, rowblk(SSM_WIDTH)),
        compiler_params=pltpu.CompilerParams(vmem_limit_bytes=VMEM_LIMIT_BYTES),
        name="ln_inproj",
    )(x, mod3, mod3, wqt, wk, wvt, ws)


def _t5_bucket_np(dist):
    max_exact = REL_BUCKETS // 2
    distf = np.maximum(dist, 1).astype(np.float32)
    large = max_exact + (np.log(distf / max_exact) / math.log(REL_MAX_DIST / max_exact)
                         * (REL_BUCKETS - max_exact)).astype(np.int32)
    large = np.minimum(large, REL_BUCKETS - 1)
    return np.where(dist < max_exact, dist, large).astype(np.int32)


def _bucket_tiles():
    kpos = np.arange(ATTN_TK)[:, None]
    qpos = np.arange(ATTN_TQ)[None, :]
    diag = qpos - kpos
    diag_b = np.where(diag >= 0, _t5_bucket_np(np.maximum(diag, 0)), -1)
    sub_b = _t5_bucket_np(diag + ATTN_TK)
    return np.stack([diag_b, sub_b]).astype(np.int32)


def _bias_kernel(tab_ref, bucket_ref, o_ref):
    h = pl.program_id(0)
    far = tab_ref[REL_BUCKETS - 1, h]
    for tile in range(2):
        b = bucket_ref[tile]
        acc = jnp.full(b.shape, NEG_INF, F32)
        for n in range(REL_BUCKETS):
            acc = jnp.where(b == n, (tab_ref[n, h] - far) * LOG2E, acc)
        o_ref[0, tile, :, 0:ATTN_TQ] = acc
        o_ref[0, tile, :, ATTN_TQ:2 * ATTN_TQ] = acc


def _bias_tiles(rel_bias):
    buckets = jnp.asarray(_bucket_tiles())
    return pl.pallas_call(
        _bias_kernel,
        out_shape=jax.ShapeDtypeStruct((ATTN_HEADS, 2, ATTN_TK, 2 * ATTN_TQ), F32),
        grid=(ATTN_HEADS,),
        in_specs=[pl.BlockSpec(memory_space=pltpu.SMEM),
                  pl.BlockSpec((2, ATTN_TK, ATTN_TQ), lambda h: (0, 0, 0))],
        out_specs=pl.BlockSpec((1, 2, ATTN_TK, 2 * ATTN_TQ), lambda h: (h, 0, 0, 0)),
        name="t5_bias_tiles",
    )(rel_bias, buckets)


def _attn_kernel(qt_ref, k_ref, vt_ref, bias_ref, lq1_ref, lk1_ref, lq2_ref, lk2_ref, g_ref,
                 o_ref, m_ref, l_ref, acc_ref, qs_ref, *, lam_init):
    qi = pl.program_id(2)
    tq, tk = ATTN_TQ, ATTN_TK
    ncol = 2 * tq // ATTN_COLS

    qt = qt_ref[0].astype(F32)
    row = lax.broadcasted_iota(jnp.int32, qt.shape, 0)
    qs_ref[...] = jnp.concatenate([jnp.where(row < HEAD_DIM, qt, 0.0),
                                   jnp.where(row >= HEAD_DIM, qt, 0.0)], axis=1).astype(BF16)

    def scores(j):
        k = k_ref[0, pl.ds(pl.multiple_of(j * tk, tk), tk), :]
        return jnp.dot(k, qs_ref[...], preferred_element_type=F32)

    def update(j, s):
        vt = vt_ref[0, :, pl.ds(pl.multiple_of(j * tk, tk), tk)]
        m_prev = m_ref[...]
        m_new = jnp.maximum(m_prev, jnp.max(s, axis=0, keepdims=True))
        alpha = jnp.exp2(m_prev - m_new)
        p = jnp.exp2(s - m_new)
        l_ref[...] = alpha * l_ref[...] + jnp.sum(p, axis=0, keepdims=True)
        acc_ref[...] = alpha * acc_ref[...] + jnp.dot(vt, p.astype(BF16), preferred_element_type=F32)
        m_ref[...] = m_new

    m_ref[...] = jnp.full(m_ref.shape, NEG_INF, F32)
    l_ref[...] = jnp.zeros_like(l_ref)
    acc_ref[...] = jnp.zeros_like(acc_ref)

    nfar = jnp.maximum(qi - 1, 0)

    def far_pair(i, carry):
        s_a = scores(2 * i)
        s_b = scores(2 * i + 1)
        update(2 * i, s_a)
        update(2 * i + 1, s_b)
        return carry

    lax.fori_loop(0, nfar // 2, far_pair, 0)

    @pl.when(nfar % 2 == 1)
    def _():
        update(nfar - 1, scores(nfar - 1))

    @pl.when(qi >= 1)
    def _():
        s_sub = scores(qi - 1)
        s_diag = scores(qi)
        update(qi - 1, s_sub + bias_ref[0, 1])
        update(qi, s_diag + bias_ref[0, 0])

    @pl.when(qi == 0)
    def _():
        update(0, scores(0) + bias_ref[0, 0])

    lam = (jnp.exp(jnp.sum(lq1_ref[...] * lk1_ref[...], axis=1, keepdims=True))
           - jnp.exp(jnp.sum(lq2_ref[...] * lk2_ref[...], axis=1, keepdims=True)) + lam_init)
    on = acc_ref[...] / l_ref[...]
    ot = on[:, 0:tq] - lam * on[:, tq:2 * tq]
    ms = jnp.mean(ot * ot, axis=0, keepdims=True)
    ot = ot * lax.rsqrt(ms + LN_EPS) * g_ref[...] * (1.0 - lam_init)
    o_ref[0] = jnp.transpose(ot).astype(BF16)


def _attention(qt, k, vt, bias_tiles, lam_q1, lam_k1, lam_q2, lam_k2, subln_g, lam_init):
    tq = ATTN_TQ
    small = lambda b, h, i: (0, 0)
    return pl.pallas_call(
        functools.partial(_attn_kernel, lam_init=lam_init),
        out_shape=jax.ShapeDtypeStruct((BATCH, SEQ, ATTN_WIDTH), BF16),
        grid=(BATCH, ATTN_HEADS, SEQ // tq),
        in_specs=[pl.BlockSpec((1, HEAD_COLS, tq), lambda b, h, i: (b, h, i)),
                  pl.BlockSpec((1, SEQ, HEAD_COLS), lambda b, h, i: (b, 0, h)),
                  pl.BlockSpec((1, HEAD_COLS, SEQ), lambda b, h, i: (b, h, 0)),
                  pl.BlockSpec((1, 2, ATTN_TK, 2 * tq), lambda b, h, i: (h, 0, 0, 0)),
                  pl.BlockSpec((1, HEAD_DIM), small), pl.BlockSpec((1, HEAD_DIM), small),
                  pl.BlockSpec((1, HEAD_DIM), small), pl.BlockSpec((1, HEAD_DIM), small),
                  pl.BlockSpec((HEAD_COLS, 1), small)],
        out_specs=pl.BlockSpec((1, tq, HEAD_COLS), lambda b, h, i: (b, i, h)),
        scratch_shapes=[pltpu.VMEM((1, 2 * tq), F32), pltpu.VMEM((1, 2 * tq), F32),
                        pltpu.VMEM((HEAD_COLS, 2 * tq), F32),
                        pltpu.VMEM((HEAD_COLS, 2 * tq), BF16)],
        compiler_params=pltpu.CompilerParams(vmem_limit_bytes=VMEM_LIMIT_BYTES),
        name="diff_attention",
    )(qt, k, vt, bias_tiles, lam_q1, lam_k1, lam_q2, lam_k2, subln_g)


def _ssm_prep_kernel(are_ref, aim_ref, ldt_ref, ebr_ref, ebi_ref, ecr_ref, eci_ref,
                     eb_ref, ec_ref, abar_ref):
    ar = are_ref[0]
    ai = aim_ref[0]
    dt = jnp.exp(ldt_ref[0])
    mag = jnp.exp(ar * dt)
    ang = ai * dt
    pr = mag * jnp.cos(ang)
    pim = mag * jnp.sin(ang)
    den = ar * ar + ai * ai
    fr = ((pr - 1.0) * ar + pim * ai) / den
    fi = (pim * ar - (pr - 1.0) * ai) / den
    ebr = ebr_ref[0]
    ebi = ebi_ref[0]
    ns = SSM_HALF_STATE
    eb_ref[0, :, 0:ns] = (fr * ebr - fi * ebi).astype(BF16)
    eb_ref[0, :, ns:2 * ns] = (fr * ebi + fi * ebr).astype(BF16)
    ec_ref[0, 0:ns, :] = ecr_ref[0].astype(BF16)
    ec_ref[0, ns:2 * ns, :] = (-eci_ref[0]).astype(BF16)
    qr, qi = pr, pim
    for _ in range(int(math.log2(SSM_CHUNK))):
        qr, qi = qr * qr - qi * qi, 2.0 * qr * qi
    abar_ref[0] = jnp.concatenate([pr, pim, qr, qi, jnp.zeros((4, ns), F32)], axis=0)


def _block_diag_half(w):
    g2 = SSM_GROUPS // 2
    w = w.reshape(2, g2, w.shape[1], w.shape[2])
    eye = jnp.eye(g2, dtype=w.dtype)
    out = w[:, :, :, None, :] * eye[None, :, None, :, None]
    return out.reshape(2, g2 * w.shape[2], g2 * w.shape[3])


def _ssm_prep(a_re, a_im, b_re, b_im, c_re, c_im, log_dt):
    ns = SSM_HALF_STATE
    row = lambda v: v.reshape(2, 1, ns)
    ldt = jnp.broadcast_to(log_dt[:, None], (SSM_GROUPS, SSM_STATE))
    ebr = _block_diag_half(jnp.transpose(b_re, (0, 2, 1)))
    ebi = _block_diag_half(jnp.transpose(b_im, (0, 2, 1)))
    ecr = _block_diag_half(jnp.transpose(c_re, (0, 2, 1)))
    eci = _block_diag_half(jnp.transpose(c_im, (0, 2, 1)))
    vec = pl.BlockSpec((1, 1, ns), lambda j: (j, 0, 0))
    return pl.pallas_call(
        _ssm_prep_kernel,
        out_shape=(jax.ShapeDtypeStruct((2, SSM_HALF, 2 * ns), BF16),
                   jax.ShapeDtypeStruct((2, 2 * ns, SSM_HALF), BF16),
                   jax.ShapeDtypeStruct((2, 8, ns), F32)),
        grid=(2,),
        in_specs=[vec, vec, vec,
                  pl.BlockSpec((1, SSM_HALF, ns), lambda j: (j, 0, 0)),
                  pl.BlockSpec((1, SSM_HALF, ns), lambda j: (j, 0, 0)),
                  pl.BlockSpec((1, ns, SSM_HALF), lambda j: (j, 0, 0)),
                  pl.BlockSpec((1, ns, SSM_HALF), lambda j: (j, 0, 0))],
        out_specs=(pl.BlockSpec((1, SSM_HALF, 2 * ns), lambda j: (j, 0, 0)),
                   pl.BlockSpec((1, 2 * ns, SSM_HALF), lambda j: (j, 0, 0)),
                   pl.BlockSpec((1, 8, ns), lambda j: (j, 0, 0))),
        name="ssm_discretise",
    )(row(a_re), row(a_im), row(ldt), ebr, ebi, ecr, eci)


def _ssm_kernel(s_ref, eb_ref, ec_ref, abar_ref, d_ref, wglu_ref, bglu_ref, y_ref, h_ref, carry_ref):
    nb, nr, nt, ns = BATCH, SSM_ROWS, SSM_CHUNK, SSM_HALF_STATE
    rows = nb * nr

    @pl.when(pl.program_id(0) == 0)
    def _():
        carry_ref[...] = jnp.zeros_like(carry_ref)

    def s_tile(t, lo, width):
        start = pl.multiple_of(t * SSM_WIDTH + lo, LANES)
        return s_ref[:, :, pl.ds(start, width)].reshape(rows, width)

    ntile = ns // LANES

    def advance(j, t):
        bu = jnp.dot(s_tile(t, j * SSM_HALF, SSM_HALF).astype(BF16), eb_ref[j],
                     preferred_element_type=F32)
        tiles = [None] * (2 * ntile)
        for k in range(ntile):
            lo = k * LANES
            ar = abar_ref[j, 0:1, lo:lo + LANES]
            ai = abar_ref[j, 1:2, lo:lo + LANES]
            hr = h_ref[j, k]
            hi = h_ref[j, ntile + k]
            new_r = ar * hr - ai * hi + bu[:, lo:lo + LANES]
            new_i = ar * hi + ai * hr + bu[:, ns + lo:ns + lo + LANES]
            h_ref[j, k] = new_r
            h_ref[j, ntile + k] = new_i
            tiles[k], tiles[ntile + k] = new_r, new_i
        return tiles

    h_ref[...] = jnp.zeros_like(h_ref)

    def local_step(t, carry):
        for j in range(2):
            advance(j, t)
        return carry

    lax.fori_loop(0, nt, local_step, 0)

    for j in range(2):
        for k in range(ntile):
            lo = k * LANES
            qr = abar_ref[j, 2:3, lo:lo + LANES]
            qi = abar_ref[j, 3:4, lo:lo + LANES]
            st_r = carry_ref[j, k]
            st_i = carry_ref[j, ntile + k]
            for c in range(nr):
                sel = pl.ds(c, nb, stride=nr)
                loc_r = h_ref[j, k, sel, :]
                loc_i = h_ref[j, ntile + k, sel, :]
                h_ref[j, k, sel, :] = st_r
                h_ref[j, ntile + k, sel, :] = st_i
                st_r, st_i = qr * st_r - qi * st_i + loc_r, qr * st_i + qi * st_r + loc_i
            carry_ref[j, k] = st_r
            carry_ref[j, ntile + k] = st_i

    def output_step(t, carry):
        ys = []
        for j in range(2):
            hcat = jnp.concatenate(advance(j, t), axis=1).astype(BF16)
            ys.append(jnp.dot(hcat, ec_ref[j], preferred_element_type=F32))
        y = jnp.concatenate(ys, axis=1) + d_ref[...] * s_tile(t, 0, SSM_WIDTH)
        y = jax.nn.gelu(y)
        z = jnp.dot(y.astype(BF16), wglu_ref[...], preferred_element_type=F32) + bglu_ref[...]
        out = (y * jax.nn.sigmoid(z)).astype(BF16)
        start = pl.multiple_of(t * SSM_WIDTH, SSM_WIDTH)
        y_ref[:, :, pl.ds(start, SSM_WIDTH)] = out.reshape(nb, nr, SSM_WIDTH)
        return carry

    lax.fori_loop(0, nt, output_step, 0)


def _ssm(s, eb, ec, abar, d_skip, w_glu_b, b_glu):
    nt, nr, ns = SSM_CHUNK, SSM_ROWS, SSM_HALF_STATE
    nchunks = SEQ // nt
    width = nt * SSM_WIDTH
    s3 = s.reshape(BATCH, nchunks, width)
    const3 = lambda i: (0, 0, 0)
    const2 = lambda i: (0, 0)
    y3 = pl.pallas_call(
        _ssm_kernel,
        out_shape=jax.ShapeDtypeStruct((BATCH, nchunks, width), BF16),
        grid=(nchunks // nr,),
        in_specs=[pl.BlockSpec((BATCH, nr, width), lambda i: (0, i, 0)),
                  pl.BlockSpec((2, SSM_HALF, 2 * ns), const3),
                  pl.BlockSpec((2, 2 * ns, SSM_HALF), const3),
                  pl.BlockSpec((2, 8, ns), const3),
                  pl.BlockSpec((1, SSM_WIDTH), const2),
                  pl.BlockSpec((SSM_WIDTH, SSM_WIDTH), const2),
                  pl.BlockSpec((1, SSM_WIDTH), const2)],
        out_specs=pl.BlockSpec((BATCH, nr, width), lambda i: (0, i, 0)),
        scratch_shapes=[pltpu.VMEM((2, 2 * ns // LANES, BATCH * nr, LANES), F32),
                        pltpu.VMEM((2, 2 * ns // LANES, BATCH, LANES), F32)],
        compiler_params=pltpu.CompilerParams(dimension_semantics=("arbitrary",),
                                             vmem_limit_bytes=VMEM_LIMIT_BYTES),
        name="s5_ssm_glu",
    )(s3, eb, ec, abar, d_skip.reshape(1, SSM_WIDTH), w_glu_b, b_glu.reshape(1, SSM_WIDTH))
    return y3.reshape(BATCH, SEQ, SSM_WIDTH)


def _ffn_kernel(x_ref, o_ref, y_ref, g1_ref, sh2_ref, sc2_ref, g2_ref, wo_ref, ln1g_ref, ln1b_ref,
                wup_ref, cw_ref, wdn_ref, ln2g_ref, ln2b_ref, out_ref,
                u_ref, f_ref, carry_ref):
    tm = FFN_ROWS

    @pl.when(pl.program_id(1) == 0)
    def _():
        carry_ref[...] = jnp.zeros_like(carry_ref)

    m = (jnp.dot(o_ref[0], wo_ref[0:ATTN_WIDTH, :], preferred_element_type=F32)
         + jnp.dot(y_ref[0], wo_ref[ATTN_WIDTH:, :], preferred_element_type=F32))
    x1 = _layernorm(DEEPNORM_ALPHA * x_ref[0] + g1_ref[0] * m) * ln1g_ref[...] + ln1b_ref[...]
    u_ref[...] = (_layernorm(x1) * (1.0 + sc2_ref[0]) + sh2_ref[0]).astype(BF16)
    out_ref[0] = x1

    def conv_cols(idx):
        hcur = jnp.dot(u_ref[...], wup_ref[idx], preferred_element_type=F32)
        ext = jnp.concatenate([carry_ref[idx], hcur], axis=0)
        carry_ref[idx] = hcur[tm - SUBLANES:tm, :]
        w = cw_ref[idx]
        return (w[2:3] * hcur + w[1:2] * ext[SUBLANES - 1:SUBLANES - 1 + tm, :]
                + w[0:1] * ext[SUBLANES - 2:SUBLANES - 2 + tm, :] + w[3:4])

    for j in range(FFN_STEPS):
        val = conv_cols(j)
        gate = conv_cols(j + FFN_STEPS)
        a = (gate * jax.nn.sigmoid(gate) * val).astype(BF16)
        part = jnp.dot(a, wdn_ref[j], preferred_element_type=F32)
        if j == 0:
            f_ref[...] = part
        else:
            f_ref[...] += part
    x1 = out_ref[0]
    out_ref[0] = (_layernorm(DEEPNORM_ALPHA * x1 + g2_ref[0] * f_ref[...]) * ln2g_ref[...]
                  + ln2b_ref[...])


def _outproj_ffn(x, o, y, mod3, w_out_b, ln1_g, ln1_b, w_up_c, conv_c, w_down_c, ln2_g, ln2_b):
    tm = FFN_ROWS
    row = lambda b, i: (b, i, 0)
    modv = lambda k: pl.BlockSpec((1, 1, D_MODEL), lambda b, i: (b, 0, k))
    c2 = lambda b, i: (0, 0)
    c3 = lambda b, i: (0, 0, 0)
    vec = pl.BlockSpec((1, D_MODEL), c2)
    return pl.pallas_call(
        _ffn_kernel,
        out_shape=jax.ShapeDtypeStruct((BATCH, SEQ, D_MODEL), F32),
        grid=(BATCH, SEQ // tm),
        in_specs=[pl.BlockSpec((1, tm, D_MODEL), row),
                  pl.BlockSpec((1, tm, ATTN_WIDTH), row),
                  pl.BlockSpec((1, tm, SSM_WIDTH), row),
                  modv(2), modv(3), modv(4), modv(5),
                  pl.BlockSpec((D_MODEL, D_MODEL), c2, pipeline_mode=pl.Buffered(1)),
                  vec, vec,
                  pl.BlockSpec((2 * FFN_STEPS, D_MODEL, FFN_COLS), c3, pipeline_mode=pl.Buffered(1)),
                  pl.BlockSpec((2 * FFN_STEPS, SUBLANES, FFN_COLS), c3),
                  pl.BlockSpec((FFN_STEPS, FFN_COLS, D_MODEL), c3, pipeline_mode=pl.Buffered(1)),
                  vec, vec],
        out_specs=pl.BlockSpec((1, tm, D_MODEL), row),
        scratch_shapes=[pltpu.VMEM((tm, D_MODEL), BF16),
                        pltpu.VMEM((tm, D_MODEL), F32),
                        pltpu.VMEM((2 * FFN_STEPS, SUBLANES, FFN_COLS), F32)],
        compiler_params=pltpu.CompilerParams(dimension_semantics=("arbitrary", "arbitrary"),
                                             vmem_limit_bytes=VMEM_LIMIT_BYTES),
        name="outproj_ffn",
    )(x, o, y, mod3, mod3, mod3, mod3, w_out_b, ln1_g, ln1_b, w_up_c, conv_c, w_down_c, ln2_g, ln2_b)


def kernel(x, c, w_ada, b_ada, w_in, lam_q1, lam_k1, lam_q2, lam_k2, subln_g, ssm_a_re, ssm_a_im,
           ssm_b_re, ssm_b_im, ssm_c_re, ssm_c_im, ssm_d, ssm_log_dt, w_glu, b_glu, w_out, ln1_g,
           ln1_b, w_up, conv_w, conv_b, w_down, ln2_g, ln2_b, rel_bias):
    bias_tiles = _bias_tiles(rel_bias)
    for layer in range(DEPTH):
        lam_init = 0.8 - 0.6 * math.exp(-0.3 * layer)
        mod3 = _modulation(c, w_ada[layer], b_ada[layer]).reshape(BATCH, 1, 6 * D_MODEL)
        qt, k, vt, s = _in_projection(x, mod3, w_in[layer].astype(BF16))
        o = _attention(qt, k, vt, bias_tiles, lam_q1[layer][None], lam_k1[layer][None],
                       lam_q2[layer][None], lam_k2[layer][None], subln_g[layer][:, None], lam_init)
        eb, ec, abar = _ssm_prep(ssm_a_re[layer], ssm_a_im[layer], ssm_b_re[layer], ssm_b_im[layer],
                                 ssm_c_re[layer], ssm_c_im[layer], ssm_log_dt[layer])
        y = _ssm(s, eb, ec, abar, ssm_d[layer], w_glu[layer].astype(BF16), b_glu[layer])
        w_up_c = jnp.transpose(w_up[layer].astype(BF16).reshape(D_MODEL, 2 * FFN_STEPS, FFN_COLS),
                               (1, 0, 2))
        conv_c = jnp.concatenate([conv_w[layer], conv_b[layer][None],
                                  jnp.zeros((SUBLANES - 4, 2 * D_FF), F32)], axis=0)
        conv_c = jnp.transpose(conv_c.reshape(SUBLANES, 2 * FFN_STEPS, FFN_COLS), (1, 0, 2))
        w_down_c = w_down[layer].astype(BF16).reshape(FFN_STEPS, FFN_COLS, D_MODEL)
        x = _outproj_ffn(x, o, y, mod3, w_out[layer].astype(BF16), ln1_g[layer][None],
                         ln1_b[layer][None], w_up_c, conv_c, w_down_c, ln2_g[layer][None],
                         ln2_b[layer][None])
    return x
```

```python
import functools
import math

import numpy as np
import jax
import jax.numpy as jnp
from jax import lax
from jax.experimental import pallas as pl
from jax.experimental.pallas import tpu as pltpu

F32 = jnp.float32
BF16 = jnp.bfloat16

D_MODEL = 1024
BATCH = 16
SEQ = 4096
ATTN_WIDTH = 512
SSM_WIDTH = 512
ATTN_HEADS = 4
HEAD_DIM = 64
HEAD_COLS = 2 * HEAD_DIM
SSM_GROUP = 16
SSM_GROUPS = 32
SSM_STATE = 64
D_FF = 2816
REL_BUCKETS = 32
REL_MAX_DIST = 128
LN_EPS = 1e-5
NEG_INF = -1e30
DEPTH = 1
DEEPNORM_ALPHA = (2 * DEPTH) ** 0.25

LANES = 128
SUBLANES = 8
MXU_DIM = 256
VMEM_LIMIT_BYTES = 56 * 1024 * 1024

PROJ_ROWS = 512
ATTN_TQ = 512
ATTN_TK = 512
V_ROWS = HEAD_COLS + 16
LOG2E = math.log2(math.e)
SSM_CHUNK = 16
SSM_ROWS = 16
SSM_HALF = SSM_WIDTH // 2
SSM_HALF_STATE = (SSM_GROUPS // 2) * SSM_STATE
FFN_ROWS = 512
FFN_COLS = 256
FFN_STEPS = D_FF // FFN_COLS


def _layernorm(x):
    mu = jnp.mean(x, axis=-1, keepdims=True)
    xc = x - mu
    var = jnp.mean(xc * xc, axis=-1, keepdims=True)
    return xc * lax.rsqrt(var + LN_EPS)


def _mod_kernel(c_ref, w_ref, b_ref, o_ref):
    c = c_ref[...]
    act = c * jax.nn.sigmoid(c)
    o_ref[...] = jnp.dot(act, w_ref[...], precision=lax.Precision.HIGHEST,
                         preferred_element_type=F32) + b_ref[...]


def _modulation(c, w_ada, b_ada):
    n = w_ada.shape[1]
    blk = D_MODEL
    return pl.pallas_call(
        _mod_kernel,
        out_shape=jax.ShapeDtypeStruct((BATCH, n), F32),
        grid=(n // blk,),
        in_specs=[pl.BlockSpec((BATCH, D_MODEL), lambda i: (0, 0)),
                  pl.BlockSpec((D_MODEL, blk), lambda i: (0, i)),
                  pl.BlockSpec((1, blk), lambda i: (0, i))],
        out_specs=pl.BlockSpec((BATCH, blk), lambda i: (0, i)),
        name="adaln_mod",
    )(c, w_ada, b_ada.reshape(1, n))


def _inproj_kernel(x_ref, sh_ref, sc_ref, wqt_ref, wk_ref, wvt_ref, ws_ref,
                   qt_ref, k_ref, vt_ref, s_ref):
    u = _layernorm(x_ref[0]) * (1.0 + sc_ref[0]) + sh_ref[0]
    ub = u.astype(BF16)
    nt = (((1,), (1,)), ((), ()))
    qt = lax.dot_general(wqt_ref[...], ub, nt, preferred_element_type=F32)
    qt_ref[0] = (qt * (HEAD_DIM ** -0.5 * LOG2E)).astype(BF16)
    k_ref[0] = jnp.dot(ub, wk_ref[...], preferred_element_type=F32).astype(BF16)
    vt = lax.dot_general(wvt_ref[...], ub, nt, preferred_element_type=F32).astype(BF16)
    ones = jnp.ones((V_ROWS - HEAD_COLS, vt.shape[1]), BF16)
    for h in range(ATTN_HEADS):
        vt_ref[0, h, 0:HEAD_COLS, :] = vt[h * HEAD_COLS:(h + 1) * HEAD_COLS, :]
        vt_ref[0, h, HEAD_COLS:V_ROWS, :] = ones
    s_ref[0] = jnp.dot(ub, ws_ref[...], preferred_element_type=F32)


def _in_projection(x, mod3, w_in_b):
    tm = PROJ_ROWS
    aw = ATTN_WIDTH
    wqt = jnp.transpose(w_in_b[:, 0:aw])
    wk = w_in_b[:, aw:2 * aw]
    wvt = jnp.transpose(w_in_b[:, 2 * aw:3 * aw])
    ws = w_in_b[:, 3 * aw:]
    const = lambda b, i: (0, 0)
    tposed = pl.BlockSpec((1, aw, tm), lambda b, i: (b, 0, i))
    rowblk = lambda w: pl.BlockSpec((1, tm, w), lambda b, i: (b, i, 0))
    return pl.pallas_call(
        _inproj_kernel,
        out_shape=(jax.ShapeDtypeStruct((BATCH, aw, SEQ), BF16),
                   jax.ShapeDtypeStruct((BATCH, SEQ, aw), BF16),
                   jax.ShapeDtypeStruct((BATCH, ATTN_HEADS, V_ROWS, SEQ), BF16),
                   jax.ShapeDtypeStruct((BATCH, SEQ, SSM_WIDTH), F32)),
        grid=(BATCH, SEQ // tm),
        in_specs=[rowblk(D_MODEL),
                  pl.BlockSpec((1, 1, D_MODEL), lambda b, i: (b, 0, 0)),
                  pl.BlockSpec((1, 1, D_MODEL), lambda b, i: (b, 0, 1)),
                  pl.BlockSpec((aw, D_MODEL), const), pl.BlockSpec((D_MODEL, aw), const),
                  pl.BlockSpec((aw, D_MODEL), const), pl.BlockSpec((D_MODEL, SSM_WIDTH), const)],
        out_specs=(tposed, rowblk(aw),
                   pl.BlockSpec((1, ATTN_HEADS, V_ROWS, tm), lambda b, i: (b, 0, 0, i)),
                   rowblk(SSM_WIDTH)),
        compiler_params=pltpu.CompilerParams(vmem_limit_bytes=VMEM_LIMIT_BYTES),
        name="ln_inproj",
    )(x, mod3, mod3, wqt, wk, wvt, ws)


def _t5_bucket_np(dist):
    max_exact = REL_BUCKETS // 2
    distf = np.maximum(dist, 1).astype(np.float32)
    large = max_exact + (np.log(distf / max_exact) / math.log(REL_MAX_DIST / max_exact)
                         * (REL_BUCKETS - max_exact)).astype(np.int32)
    large = np.minimum(large, REL_BUCKETS - 1)
    return np.where(dist < max_exact, dist, large).astype(np.int32)


def _bucket_tiles():
    kpos = np.arange(ATTN_TK)[:, None]
    qpos = np.arange(ATTN_TQ)[None, :]
    diag = qpos - kpos
    diag_b = np.where(diag >= 0, _t5_bucket_np(np.maximum(diag, 0)), -1)
    sub_b = _t5_bucket_np(diag + ATTN_TK)
    return np.stack([diag_b, sub_b]).astype(np.int32)


def _bias_kernel(tab_ref, bucket_ref, o_ref):
    h = pl.program_id(0)
    far = tab_ref[REL_BUCKETS - 1, h]
    for tile in range(2):
        b = bucket_ref[tile]
        acc = jnp.full(b.shape, NEG_INF, F32)
        for n in range(REL_BUCKETS):
            acc = jnp.where(b == n, (tab_ref[n, h] - far) * LOG2E, acc)
        o_ref[0, tile, :, 0:ATTN_TQ] = acc
        o_ref[0, tile, :, ATTN_TQ:2 * ATTN_TQ] = acc


def _bias_tiles(rel_bias):
    buckets = jnp.asarray(_bucket_tiles())
    return pl.pallas_call(
        _bias_kernel,
        out_shape=jax.ShapeDtypeStruct((ATTN_HEADS, 2, ATTN_TK, 2 * ATTN_TQ), F32),
        grid=(ATTN_HEADS,),
        in_specs=[pl.BlockSpec(memory_space=pltpu.SMEM),
                  pl.BlockSpec((2, ATTN_TK, ATTN_TQ), lambda h: (0, 0, 0))],
        out_specs=pl.BlockSpec((1, 2, ATTN_TK, 2 * ATTN_TQ), lambda h: (h, 0, 0, 0)),
        name="t5_bias_tiles",
    )(rel_bias, buckets)


def _attn_kernel(qt_ref, k_ref, vt_ref, bias_ref, lq1_ref, lk1_ref, lq2_ref, lk2_ref, g_ref,
                 o_ref, m_ref, acc_ref, qs_ref, *, lam_init):
    qi = pl.program_id(2)
    tq, tk = ATTN_TQ, ATTN_TK

    qt = qt_ref[0].astype(F32)
    row = lax.broadcasted_iota(jnp.int32, qt.shape, 0)
    qs_ref[...] = jnp.concatenate([jnp.where(row < HEAD_DIM, qt, 0.0),
                                   jnp.where(row >= HEAD_DIM, qt, 0.0)], axis=1).astype(BF16)

    def scores(j):
        k = k_ref[0, pl.ds(pl.multiple_of(j * tk, tk), tk), :]
        return jnp.dot(k, qs_ref[...], preferred_element_type=F32)

    def update(j, s):
        vt = vt_ref[0, 0, :, pl.ds(pl.multiple_of(j * tk, tk), tk)]
        m_prev = m_ref[...]
        m_new = jnp.maximum(m_prev, jnp.max(s, axis=0, keepdims=True))
        alpha = jnp.exp2(m_prev - m_new)
        p = jnp.exp2(s - m_new).astype(BF16)
        acc_ref[...] = alpha * acc_ref[...] + jnp.dot(vt, p, preferred_element_type=F32)
        m_ref[...] = m_new

    m_ref[...] = jnp.full(m_ref.shape, NEG_INF, F32)
    acc_ref[...] = jnp.zeros_like(acc_ref)

    nfar = jnp.maximum(qi - 1, 0)

    def far_pair(i, carry):
        s_a = scores(2 * i)
        s_b = scores(2 * i + 1)
        update(2 * i, s_a)
        update(2 * i + 1, s_b)
        return carry

    lax.fori_loop(0, nfar // 2, far_pair, 0)

    @pl.when(nfar % 2 == 1)
    def _():
        update(nfar - 1, scores(nfar - 1))

    @pl.when(qi >= 1)
    def _():
        s_sub = scores(qi - 1)
        s_diag = scores(qi)
        update(qi - 1, s_sub + bias_ref[0, 1])
        update(qi, s_diag + bias_ref[0, 0])

    @pl.when(qi == 0)
    def _():
        update(0, scores(0) + bias_ref[0, 0])

    lam = (jnp.exp(jnp.sum(lq1_ref[...] * lk1_ref[...], axis=1, keepdims=True))
           - jnp.exp(jnp.sum(lq2_ref[...] * lk2_ref[...], axis=1, keepdims=True)) + lam_init)
    on = acc_ref[0:HEAD_COLS, :] / acc_ref[HEAD_COLS:HEAD_COLS + 1, :]
    ot = on[:, 0:tq] - lam * on[:, tq:2 * tq]
    ms = jnp.mean(ot * ot, axis=0, keepdims=True)
    ot = ot * lax.rsqrt(ms + LN_EPS) * g_ref[...] * (1.0 - lam_init)
    o_ref[0] = jnp.transpose(ot).astype(BF16)


def _attention(qt, k, vt, bias_tiles, lam_q1, lam_k1, lam_q2, lam_k2, subln_g, lam_init):
    tq = ATTN_TQ
    small = lambda b, h, i: (0, 0)
    return pl.pallas_call(
        functools.partial(_attn_kernel, lam_init=lam_init),
        out_shape=jax.ShapeDtypeStruct((BATCH, SEQ, ATTN_WIDTH), BF16),
        grid=(BATCH, ATTN_HEADS, SEQ // tq),
        in_specs=[pl.BlockSpec((1, HEAD_COLS, tq), lambda b, h, i: (b, h, i)),
                  pl.BlockSpec((1, SEQ, HEAD_COLS), lambda b, h, i: (b, 0, h)),
                  pl.BlockSpec((1, 1, V_ROWS, SEQ), lambda b, h, i: (b, h, 0, 0)),
                  pl.BlockSpec((1, 2, ATTN_TK, 2 * tq), lambda b, h, i: (h, 0, 0, 0)),
                  pl.BlockSpec((1, HEAD_DIM), small), pl.BlockSpec((1, HEAD_DIM), small),
                  pl.BlockSpec((1, HEAD_DIM), small), pl.BlockSpec((1, HEAD_DIM), small),
                  pl.BlockSpec((HEAD_COLS, 1), small)],
        out_specs=pl.BlockSpec((1, tq, HEAD_COLS), lambda b, h, i: (b, i, h)),
        scratch_shapes=[pltpu.VMEM((1, 2 * tq), F32),
                        pltpu.VMEM((V_ROWS, 2 * tq), F32),
                        pltpu.VMEM((HEAD_COLS, 2 * tq), BF16)],
        compiler_params=pltpu.CompilerParams(vmem_limit_bytes=VMEM_LIMIT_BYTES),
        name="diff_attention",
    )(qt, k, vt, bias_tiles, lam_q1, lam_k1, lam_q2, lam_k2, subln_g)


def _ssm_prep_kernel(are_ref, aim_ref, ldt_ref, ebr_ref, ebi_ref, ecr_ref, eci_ref,
                     eb_ref, ec_ref, abar_ref):
    ar = are_ref[0]
    ai = aim_ref[0]
    dt = jnp.exp(ldt_ref[0])
    mag = jnp.exp(ar * dt)
    ang = ai * dt
    pr = mag * jnp.cos(ang)
    pim = mag * jnp.sin(ang)
    den = ar * ar + ai * ai
    fr = ((pr - 1.0) * ar + pim * ai) / den
    fi = (pim * ar - (pr - 1.0) * ai) / den
    ebr = ebr_ref[0]
    ebi = ebi_ref[0]
    ns = SSM_HALF_STATE
    eb_ref[0, :, 0:ns] = (fr * ebr - fi * ebi).astype(BF16)
    eb_ref[0, :, ns:2 * ns] = (fr * ebi + fi * ebr).astype(BF16)
    ec_ref[0, 0:ns, :] = ecr_ref[0].astype(BF16)
    ec_ref[0, ns:2 * ns, :] = (-eci_ref[0]).astype(BF16)
    qr, qi = pr, pim
    for _ in range(int(math.log2(SSM_CHUNK))):
        qr, qi = qr * qr - qi * qi, 2.0 * qr * qi
    abar_ref[0] = jnp.concatenate([pr, pim, qr, qi, jnp.zeros((4, ns), F32)], axis=0)


def _block_diag_half(w):
    g2 = SSM_GROUPS // 2
    w = w.reshape(2, g2, w.shape[1], w.shape[2])
    eye = jnp.eye(g2, dtype=w.dtype)
    out = w[:, :, :, None, :] * eye[None, :, None, :, None]
    return out.reshape(2, g2 * w.shape[2], g2 * w.shape[3])


def _ssm_prep(a_re, a_im, b_re, b_im, c_re, c_im, log_dt):
    ns = SSM_HALF_STATE
    row = lambda v: v.reshape(2, 1, ns)
    ldt = jnp.broadcast_to(log_dt[:, None], (SSM_GROUPS, SSM_STATE))
    ebr = _block_diag_half(jnp.transpose(b_re, (0, 2, 1)))
    ebi = _block_diag_half(jnp.transpose(b_im, (0, 2, 1)))
    ecr = _block_diag_half(jnp.transpose(c_re, (0, 2, 1)))
    eci = _block_diag_half(jnp.transpose(c_im, (0, 2, 1)))
    vec = pl.BlockSpec((1, 1, ns), lambda j: (j, 0, 0))
    return pl.pallas_call(
        _ssm_prep_kernel,
        out_shape=(jax.ShapeDtypeStruct((2, SSM_HALF, 2 * ns), BF16),
                   jax.ShapeDtypeStruct((2, 2 * ns, SSM_HALF), BF16),
                   jax.ShapeDtypeStruct((2, 8, ns), F32)),
        grid=(2,),
        in_specs=[vec, vec, vec,
                  pl.BlockSpec((1, SSM_HALF, ns), lambda j: (j, 0, 0)),
                  pl.BlockSpec((1, SSM_HALF, ns), lambda j: (j, 0, 0)),
                  pl.BlockSpec((1, ns, SSM_HALF), lambda j: (j, 0, 0)),
                  pl.BlockSpec((1, ns, SSM_HALF), lambda j: (j, 0, 0))],
        out_specs=(pl.BlockSpec((1, SSM_HALF, 2 * ns), lambda j: (j, 0, 0)),
                   pl.BlockSpec((1, 2 * ns, SSM_HALF), lambda j: (j, 0, 0)),
                   pl.BlockSpec((1, 8, ns), lambda j: (j, 0, 0))),
        name="ssm_discretise",
    )(row(a_re), row(a_im), row(ldt), ebr, ebi, ecr, eci)


def _ssm_kernel(s0_ref, s1_ref, s2_ref, s3_ref, eb_ref, ec_ref, abar_ref, d_ref, wglu_ref, bglu_ref,
                y0_ref, y1_ref, y2_ref, y3_ref, h_ref, carry_ref):
    nb, nr, nt, ns = BATCH, SSM_ROWS, SSM_CHUNK, SSM_HALF_STATE
    rows = nb * nr
    s_refs = (s0_ref, s1_ref, s2_ref, s3_ref)
    y_refs = (y0_ref, y1_ref, y2_ref, y3_ref)

    @pl.when(pl.program_id(0) == 0)
    def _():
        carry_ref[...] = jnp.zeros_like(carry_ref)

    def s_tile(t, lo, width):
        step = pl.ds(t, nr, stride=nt)
        slabs = [s_refs[k][:, step, :].reshape(rows, LANES)
                 for k in range(lo // LANES, (lo + width) // LANES)]
        return jnp.concatenate(slabs, axis=1)

    ntile = ns // LANES

    def advance(j, t):
        bu = jnp.dot(s_tile(t, j * SSM_HALF, SSM_HALF).astype(BF16), eb_ref[j],
                     preferred_element_type=F32)
        tiles = [None] * (2 * ntile)
        for k in range(ntile):
            lo = k * LANES
            ar = abar_ref[j, 0:1, lo:lo + LANES]
            ai = abar_ref[j, 1:2, lo:lo + LANES]
            hr = h_ref[j, k]
            hi = h_ref[j, ntile + k]
            new_r = ar * hr - ai * hi + bu[:, lo:lo + LANES]
            new_i = ar * hi + ai * hr + bu[:, ns + lo:ns + lo + LANES]
            h_ref[j, k] = new_r
            h_ref[j, ntile + k] = new_i
            tiles[k], tiles[ntile + k] = new_r, new_i
        return tiles

    h_ref[...] = jnp.zeros_like(h_ref)

    def local_step(t, carry):
        for j in range(2):
            advance(j, t)
        return carry

    lax.fori_loop(0, nt, local_step, 0)

    for j in range(2):
        for k in range(ntile):
            lo = k * LANES
            qr = abar_ref[j, 2:3, lo:lo + LANES]
            qi = abar_ref[j, 3:4, lo:lo + LANES]
            st_r = carry_ref[j, k]
            st_i = carry_ref[j, ntile + k]
            for c in range(nr):
                sel = pl.ds(c, nb, stride=nr)
                loc_r = h_ref[j, k, sel, :]
                loc_i = h_ref[j, ntile + k, sel, :]
                h_ref[j, k, sel, :] = st_r
                h_ref[j, ntile + k, sel, :] = st_i
                st_r, st_i = qr * st_r - qi * st_i + loc_r, qr * st_i + qi * st_r + loc_i
            carry_ref[j, k] = st_r
            carry_ref[j, ntile + k] = st_i

    def output_step(t, carry):
        ys = []
        for j in range(2):
            hcat = jnp.concatenate(advance(j, t), axis=1).astype(BF16)
            ys.append(jnp.dot(hcat, ec_ref[j], preferred_element_type=F32))
        y = jnp.concatenate(ys, axis=1) + d_ref[...] * s_tile(t, 0, SSM_WIDTH)
        y = jax.nn.gelu(y)
        z = jnp.dot(y.astype(BF16), wglu_ref[...], preferred_element_type=F32) + bglu_ref[...]
        out = y * jax.nn.sigmoid(z)
        step = pl.ds(t, nr, stride=nt)
        for k in range(SSM_WIDTH // LANES):
            y_refs[k][:, step, :] = out[:, k * LANES:(k + 1) * LANES].reshape(nb, nr, LANES)
        return carry

    lax.fori_loop(0, nt, output_step, 0)


def _ssm(s, eb, ec, abar, d_skip, w_glu_b, b_glu):
    nt, nr, ns = SSM_CHUNK, SSM_ROWS, SSM_HALF_STATE
    nslab = SSM_WIDTH // LANES
    const3 = lambda i: (0, 0, 0)
    const2 = lambda i: (0, 0)
    once = dict(pipeline_mode=pl.Buffered(1))
    slab = lambda k: pl.BlockSpec((BATCH, nr * nt, LANES), lambda i: (0, i, k))
    return pl.pallas_call(
        _ssm_kernel,
        out_shape=tuple(jax.ShapeDtypeStruct((BATCH, SEQ, LANES), F32) for _ in range(nslab)),
        grid=(SEQ // (nr * nt),),
        in_specs=[slab(k) for k in range(nslab)] + [
            pl.BlockSpec((2, SSM_HALF, 2 * ns), const3, **once),
            pl.BlockSpec((2, 2 * ns, SSM_HALF), const3, **once),
            pl.BlockSpec((2, 8, ns), const3),
            pl.BlockSpec((1, SSM_WIDTH), const2),
            pl.BlockSpec((SSM_WIDTH, SSM_WIDTH), const2),
            pl.BlockSpec((1, SSM_WIDTH), const2)],
        out_specs=tuple(slab(0) for _ in range(nslab)),
        scratch_shapes=[pltpu.VMEM((2, 2 * ns // LANES, BATCH * nr, LANES), F32),
                        pltpu.VMEM((2, 2 * ns // LANES, BATCH, LANES), F32)],
        compiler_params=pltpu.CompilerParams(dimension_semantics=("arbitrary",),
                                             vmem_limit_bytes=VMEM_LIMIT_BYTES),
        name="s5_ssm_glu",
    )(s, s, s, s, eb, ec, abar, d_skip.reshape(1, SSM_WIDTH), w_glu_b, b_glu.reshape(1, SSM_WIDTH))


def _ffn_kernel(x_ref, o_ref, y0_ref, y1_ref, y2_ref, y3_ref, g1_ref, sh2_ref, sc2_ref, g2_ref, wo_ref,
                ln1g_ref, ln1b_ref, wup_ref, cw_ref, wdn_ref, ln2g_ref, ln2b_ref, out_ref,
                u_ref, f_ref, carry_ref):
    tm = FFN_ROWS

    @pl.when(pl.program_id(1) == 0)
    def _():
        carry_ref[...] = jnp.zeros_like(carry_ref)

    y = jnp.concatenate([y0_ref[0], y1_ref[0], y2_ref[0], y3_ref[0]], axis=1).astype(BF16)
    m = (jnp.dot(o_ref[0], wo_ref[0:ATTN_WIDTH, :], preferred_element_type=F32)
         + jnp.dot(y, wo_ref[ATTN_WIDTH:, :], preferred_element_type=F32))
    x1 = _layernorm(DEEPNORM_ALPHA * x_ref[0] + g1_ref[0] * m) * ln1g_ref[...] + ln1b_ref[...]
    u_ref[...] = (_layernorm(x1) * (1.0 + sc2_ref[0]) + sh2_ref[0]).astype(BF16)
    out_ref[0] = x1

    def conv_cols(idx):
        hcur = jnp.dot(u_ref[...], wup_ref[idx], preferred_element_type=F32)
        ext = jnp.concatenate([carry_ref[idx], hcur], axis=0)
        carry_ref[idx] = hcur[tm - SUBLANES:tm, :]
        w = cw_ref[idx]
        return (w[2:3] * hcur + w[1:2] * ext[SUBLANES - 1:SUBLANES - 1 + tm, :]
                + w[0:1] * ext[SUBLANES - 2:SUBLANES - 2 + tm, :] + w[3:4])

    for j in range(FFN_STEPS):
        val = conv_cols(j)
        gate = conv_cols(j + FFN_STEPS)
        a = (gate * jax.nn.sigmoid(gate) * val).astype(BF16)
        part = jnp.dot(a, wdn_ref[j], preferred_element_type=F32)
        if j == 0:
            f_ref[...] = part
        else:
            f_ref[...] += part
    x1 = out_ref[0]
    out_ref[0] = (_layernorm(DEEPNORM_ALPHA * x1 + g2_ref[0] * f_ref[...]) * ln2g_ref[...]
                  + ln2b_ref[...])


def _outproj_ffn(x, o, y_slabs, mod3, w_out_b, ln1_g, ln1_b, w_up_c, conv_c, w_down_c, ln2_g, ln2_b):
    tm = FFN_ROWS
    row = lambda b, i: (b, i, 0)
    modv = lambda k: pl.BlockSpec((1, 1, D_MODEL), lambda b, i: (b, 0, k))
    c2 = lambda b, i: (0, 0)
    c3 = lambda b, i: (0, 0, 0)
    vec = pl.BlockSpec((1, D_MODEL), c2)
    return pl.pallas_call(
        _ffn_kernel,
        out_shape=jax.ShapeDtypeStruct((BATCH, SEQ, D_MODEL), F32),
        grid=(BATCH, SEQ // tm),
        in_specs=[pl.BlockSpec((1, tm, D_MODEL), row),
                  pl.BlockSpec((1, tm, ATTN_WIDTH), row),
                  *[pl.BlockSpec((1, tm, LANES), row) for _ in range(SSM_WIDTH // LANES)],
                  modv(2), modv(3), modv(4), modv(5),
                  pl.BlockSpec((D_MODEL, D_MODEL), c2, pipeline_mode=pl.Buffered(1)),
                  vec, vec,
                  pl.BlockSpec((2 * FFN_STEPS, D_MODEL, FFN_COLS), c3, pipeline_mode=pl.Buffered(1)),
                  pl.BlockSpec((2 * FFN_STEPS, SUBLANES, FFN_COLS), c3),
                  pl.BlockSpec((FFN_STEPS, FFN_COLS, D_MODEL), c3, pipeline_mode=pl.Buffered(1)),
                  vec, vec],
        out_specs=pl.BlockSpec((1, tm, D_MODEL), row),
        scratch_shapes=[pltpu.VMEM((tm, D_MODEL), BF16),
                        pltpu.VMEM((tm, D_MODEL), F32),
                        pltpu.VMEM((2 * FFN_STEPS, SUBLANES, FFN_COLS), F32)],
        compiler_params=pltpu.CompilerParams(dimension_semantics=("arbitrary", "arbitrary"),
                                             vmem_limit_bytes=VMEM_LIMIT_BYTES),
        name="outproj_ffn",
    )(x, o, *y_slabs, mod3, mod3, mod3, mod3, w_out_b, ln1_g, ln1_b, w_up_c, conv_c, w_down_c, ln2_g, ln2_b)


def kernel(x, c, w_ada, b_ada, w_in, lam_q1, lam_k1, lam_q2, lam_k2, subln_g, ssm_a_re, ssm_a_im,
           ssm_b_re, ssm_b_im, ssm_c_re, ssm_c_im, ssm_d, ssm_log_dt, w_glu, b_glu, w_out, ln1_g,
           ln1_b, w_up, conv_w, conv_b, w_down, ln2_g, ln2_b, rel_bias):
    bias_tiles = _bias_tiles(rel_bias)
    for layer in range(DEPTH):
        lam_init = 0.8 - 0.6 * math.exp(-0.3 * layer)
        mod3 = _modulation(c, w_ada[layer], b_ada[layer]).reshape(BATCH, 1, 6 * D_MODEL)
        qt, k, vt, s = _in_projection(x, mod3, w_in[layer].astype(BF16))
        o = _attention(qt, k, vt, bias_tiles, lam_q1[layer][None], lam_k1[layer][None],
                       lam_q2[layer][None], lam_k2[layer][None], subln_g[layer][:, None], lam_init)
        eb, ec, abar = _ssm_prep(ssm_a_re[layer], ssm_a_im[layer], ssm_b_re[layer], ssm_b_im[layer],
                                 ssm_c_re[layer], ssm_c_im[layer], ssm_log_dt[layer])
        y = _ssm(s, eb, ec, abar, ssm_d[layer], w_glu[layer].astype(BF16), b_glu[layer])
        w_up_c = jnp.transpose(w_up[layer].astype(BF16).reshape(D_MODEL, 2 * FFN_STEPS, FFN_COLS),
                               (1, 0, 2))
        conv_c = jnp.concatenate([conv_w[layer], conv_b[layer][None],
                                  jnp.zeros((SUBLANES - 4, 2 * D_FF), F32)], axis=0)
        conv_c = jnp.transpose(conv_c.reshape(SUBLANES, 2 * FFN_STEPS, FFN_COLS), (1, 0, 2))
        w_down_c = w_down[layer].astype(BF16).reshape(FFN_STEPS, FFN_COLS, D_MODEL)
        x = _outproj_ffn(x, o, y, mod3, w_out[layer].astype(BF16), ln1_g[layer][None],
                         ln1_b[layer][None], w_up_c, conv_c, w_down_c, ln2_g[layer][None],
                         ln2_b[layer][None])
    return x
```

```python
import functools
import math

import numpy as np
import jax
import jax.numpy as jnp
from jax import lax
from jax.experimental import pallas as pl
from jax.experimental.pallas import tpu as pltpu

F32 = jnp.float32
BF16 = jnp.bfloat16

D_MODEL = 1024
BATCH = 16
SEQ = 4096
ATTN_WIDTH = 512
SSM_WIDTH = 512
ATTN_HEADS = 4
HEAD_DIM = 64
HEAD_COLS = 2 * HEAD_DIM
SSM_GROUP = 16
SSM_GROUPS = 32
SSM_STATE = 64
D_FF = 2816
REL_BUCKETS = 32
REL_MAX_DIST = 128
LN_EPS = 1e-5
NEG_INF = -1e30
DEPTH = 1
DEEPNORM_ALPHA = (2 * DEPTH) ** 0.25

LANES = 128
SUBLANES = 8
MXU_DIM = 256
VMEM_LIMIT_BYTES = 56 * 1024 * 1024

PROJ_ROWS = 512
ATTN_TQ = 512
ATTN_TK = 512
V_ROWS = HEAD_COLS + 16
LOG2E = math.log2(math.e)
SSM_CHUNK = 16
SSM_ROWS = 16
SSM_HALF = SSM_WIDTH // 2
SSM_HALF_STATE = (SSM_GROUPS // 2) * SSM_STATE
FFN_ROWS = 512
FFN_COLS = 256
FFN_STEPS = D_FF // FFN_COLS


def _layernorm(x):
    mu = jnp.mean(x, axis=-1, keepdims=True)
    xc = x - mu
    var = jnp.mean(xc * xc, axis=-1, keepdims=True)
    return xc * lax.rsqrt(var + LN_EPS)


def _mod_kernel(c_ref, w_ref, b_ref, o_ref):
    c = c_ref[...]
    act = c * jax.nn.sigmoid(c)
    o_ref[...] = jnp.dot(act, w_ref[...], precision=lax.Precision.HIGHEST,
                         preferred_element_type=F32) + b_ref[...]


def _modulation(c, w_ada, b_ada):
    n = w_ada.shape[1]
    blk = D_MODEL
    return pl.pallas_call(
        _mod_kernel,
        out_shape=jax.ShapeDtypeStruct((BATCH, n), F32),
        grid=(n // blk,),
        in_specs=[pl.BlockSpec((BATCH, D_MODEL), lambda i: (0, 0)),
                  pl.BlockSpec((D_MODEL, blk), lambda i: (0, i)),
                  pl.BlockSpec((1, blk), lambda i: (0, i))],
        out_specs=pl.BlockSpec((BATCH, blk), lambda i: (0, i)),
        name="adaln_mod",
    )(c, w_ada, b_ada.reshape(1, n))


def _inproj_kernel(x_ref, sh_ref, sc_ref, wqt_ref, wk_ref, wvt_ref, ws_ref,
                   qt_ref, k_ref, vt_ref, s_ref):
    u = _layernorm(x_ref[0]) * (1.0 + sc_ref[0]) + sh_ref[0]
    ub = u.astype(BF16)
    nt = (((1,), (1,)), ((), ()))
    qt = lax.dot_general(wqt_ref[...], ub, nt, preferred_element_type=F32)
    qt_ref[0] = (qt * (HEAD_DIM ** -0.5 * LOG2E)).astype(BF16)
    k_ref[0] = jnp.dot(ub, wk_ref[...], preferred_element_type=F32).astype(BF16)
    vt = lax.dot_general(wvt_ref[...], ub, nt, preferred_element_type=F32).astype(BF16)
    ones = jnp.ones((V_ROWS - HEAD_COLS, vt.shape[1]), BF16)
    for h in range(ATTN_HEADS):
        vt_ref[0, h, 0:HEAD_COLS, :] = vt[h * HEAD_COLS:(h + 1) * HEAD_COLS, :]
        vt_ref[0, h, HEAD_COLS:V_ROWS, :] = ones
    s_ref[0] = jnp.dot(ub, ws_ref[...], preferred_element_type=F32)


def _in_projection(x, mod3, w_in_b):
    tm = PROJ_ROWS
    aw = ATTN_WIDTH
    wqt = jnp.transpose(w_in_b[:, 0:aw])
    wk = w_in_b[:, aw:2 * aw]
    wvt = jnp.transpose(w_in_b[:, 2 * aw:3 * aw])
    ws = w_in_b[:, 3 * aw:]
    const = lambda b, i: (0, 0)
    tposed = pl.BlockSpec((1, aw, tm), lambda b, i: (b, 0, i))
    rowblk = lambda w: pl.BlockSpec((1, tm, w), lambda b, i: (b, i, 0))
    return pl.pallas_call(
        _inproj_kernel,
        out_shape=(jax.ShapeDtypeStruct((BATCH, aw, SEQ), BF16),
                   jax.ShapeDtypeStruct((BATCH, SEQ, aw), BF16),
                   jax.ShapeDtypeStruct((BATCH, ATTN_HEADS, V_ROWS, SEQ), BF16),
                   jax.ShapeDtypeStruct((BATCH, SEQ, SSM_WIDTH), F32)),
        grid=(BATCH, SEQ // tm),
        in_specs=[rowblk(D_MODEL),
                  pl.BlockSpec((1, 1, D_MODEL), lambda b, i: (b, 0, 0)),
                  pl.BlockSpec((1, 1, D_MODEL), lambda b, i: (b, 0, 1)),
                  pl.BlockSpec((aw, D_MODEL), const), pl.BlockSpec((D_MODEL, aw), const),
                  pl.BlockSpec((aw, D_MODEL), const), pl.BlockSpec((D_MODEL, SSM_WIDTH), const)],
        out_specs=(tposed, rowblk(aw),
                   pl.BlockSpec((1, ATTN_HEADS, V_ROWS, tm), lambda b, i: (b, 0, 0, i)),
                   rowblk(SSM_WIDTH)),
        compiler_params=pltpu.CompilerParams(vmem_limit_bytes=VMEM_LIMIT_BYTES),
        name="ln_inproj",
    )(x, mod3, mod3, wqt, wk, wvt, ws)


def _t5_bucket_np(dist):
    max_exact = REL_BUCKETS // 2
    distf = np.maximum(dist, 1).astype(np.float32)
    large = max_exact + (np.log(distf / max_exact) / math.log(REL_MAX_DIST / max_exact)
                         * (REL_BUCKETS - max_exact)).astype(np.int32)
    large = np.minimum(large, REL_BUCKETS - 1)
    return np.where(dist < max_exact, dist, large).astype(np.int32)


def _bucket_tiles():
    kpos = np.arange(ATTN_TK)[:, None]
    qpos = np.arange(ATTN_TQ)[None, :]
    diag = qpos - kpos
    diag_b = np.where(diag >= 0, _t5_bucket_np(np.maximum(diag, 0)), -1)
    sub_b = _t5_bucket_np(diag + ATTN_TK)
    return np.stack([diag_b, sub_b]).astype(np.int32)


def _bias_kernel(tab_ref, bucket_ref, o_ref):
    h = pl.program_id(0)
    far = tab_ref[REL_BUCKETS - 1, h]
    for tile in range(2):
        b = bucket_ref[tile]
        acc = jnp.full(b.shape, NEG_INF, F32)
        for n in range(REL_BUCKETS):
            acc = jnp.where(b == n, (tab_ref[n, h] - far) * LOG2E, acc)
        o_ref[0, tile] = acc


def _bias_tiles(rel_bias):
    buckets = jnp.asarray(_bucket_tiles())
    return pl.pallas_call(
        _bias_kernel,
        out_shape=jax.ShapeDtypeStruct((ATTN_HEADS, 2, ATTN_TK, ATTN_TQ), F32),
        grid=(ATTN_HEADS,),
        in_specs=[pl.BlockSpec(memory_space=pltpu.SMEM),
                  pl.BlockSpec((2, ATTN_TK, ATTN_TQ), lambda h: (0, 0, 0))],
        out_specs=pl.BlockSpec((1, 2, ATTN_TK, ATTN_TQ), lambda h: (h, 0, 0, 0)),
        name="t5_bias_tiles",
    )(rel_bias, buckets)


def _attn_kernel(qt_ref, k_ref, vt_ref, bias_ref, lq1_ref, lk1_ref, lq2_ref, lk2_ref, g_ref,
                 o_ref, m_ref, acc_ref, qs_ref, *, lam_init):
    qi = pl.program_id(2)
    tq, tk = ATTN_TQ, ATTN_TK

    qt = qt_ref[0].astype(F32)
    row = lax.broadcasted_iota(jnp.int32, qt.shape, 0)
    qs_ref[0] = jnp.where(row < HEAD_DIM, qt, 0.0).astype(BF16)
    qs_ref[1] = jnp.where(row >= HEAD_DIM, qt, 0.0).astype(BF16)

    def scores(j):
        k = k_ref[0, pl.ds(pl.multiple_of(j * tk, tk), tk), :]
        return [jnp.dot(k, qs_ref[c], preferred_element_type=F32) for c in range(2)]

    def update(j, ss):
        vt = vt_ref[0, 0, :, pl.ds(pl.multiple_of(j * tk, tk), tk)]
        for c in range(2):
            s = ss[c]
            m_prev = m_ref[c]
            m_new = jnp.maximum(m_prev, jnp.max(s, axis=0, keepdims=True))
            alpha = jnp.exp2(m_prev - m_new)
            p = jnp.exp2(s - m_new).astype(BF16)
            acc_ref[c] = alpha * acc_ref[c] + jnp.dot(vt, p, preferred_element_type=F32)
            m_ref[c] = m_new

    m_ref[...] = jnp.full(m_ref.shape, NEG_INF, F32)
    acc_ref[...] = jnp.zeros_like(acc_ref)

    nfar = jnp.maximum(qi - 1, 0)

    def far_pair(i, carry):
        s_a = scores(2 * i)
        s_b = scores(2 * i + 1)
        update(2 * i, s_a)
        update(2 * i + 1, s_b)
        return carry

    lax.fori_loop(0, nfar // 2, far_pair, 0)

    def far_single(i, carry):
        update(nfar - 1, scores(nfar - 1))
        return carry

    lax.fori_loop(0, nfar % 2, far_single, 0)

    half = tk // 2

    def diag_scores():
        parts = []
        for part in range(2):
            koff = pl.multiple_of(qi * tk + part * half, half)
            k = k_ref[0, pl.ds(koff, half), :]
            q0 = part * half
            parts.append([jnp.dot(k, qs_ref[c, :, q0:tq], preferred_element_type=F32)
                          + bias_ref[0, 0, part * half:(part + 1) * half, q0:tq] for c in range(2)])
        return parts

    def diag_update(parts):
        for part in range(2):
            koff = pl.multiple_of(qi * tk + part * half, half)
            vt = vt_ref[0, 0, :, pl.ds(koff, half)]
            q0 = part * half
            for c in range(2):
                s = parts[part][c]
                m_prev = m_ref[c, :, q0:tq]
                m_new = jnp.maximum(m_prev, jnp.max(s, axis=0, keepdims=True))
                alpha = jnp.exp2(m_prev - m_new)
                p = jnp.exp2(s - m_new).astype(BF16)
                acc_ref[c, :, q0:tq] = (alpha * acc_ref[c, :, q0:tq]
                                        + jnp.dot(vt, p, preferred_element_type=F32))
                m_ref[c, :, q0:tq] = m_new

    def sub_and_diag(i, carry):
        s_sub = [s + bias_ref[0, 1] for s in scores(qi - 1)]
        parts = diag_scores()
        update(qi - 1, s_sub)
        diag_update(parts)
        return carry

    lax.fori_loop(0, jnp.minimum(qi, 1), sub_and_diag, 0)

    def diag_only(i, carry):
        diag_update(diag_scores())
        return carry

    lax.fori_loop(0, 1 - jnp.minimum(qi, 1), diag_only, 0)

    lam = (jnp.exp(jnp.sum(lq1_ref[...] * lk1_ref[...], axis=1, keepdims=True))
           - jnp.exp(jnp.sum(lq2_ref[...] * lk2_ref[...], axis=1, keepdims=True)) + lam_init)
    on = [acc_ref[c, 0:HEAD_COLS, :] / acc_ref[c, HEAD_COLS:HEAD_COLS + 1, :] for c in range(2)]
    ot = on[0] - lam * on[1]
    ms = jnp.mean(ot * ot, axis=0, keepdims=True)
    ot = ot * lax.rsqrt(ms + LN_EPS) * g_ref[...] * (1.0 - lam_init)
    o_ref[0] = jnp.transpose(ot).astype(BF16)


def _attention(qt, k, vt, bias_tiles, lam_q1, lam_k1, lam_q2, lam_k2, subln_g, lam_init):
    tq = ATTN_TQ
    small = lambda b, h, i: (0, 0)
    return pl.pallas_call(
        functools.partial(_attn_kernel, lam_init=lam_init),
        out_shape=jax.ShapeDtypeStruct((BATCH, SEQ, ATTN_WIDTH), BF16),
        grid=(BATCH, ATTN_HEADS, SEQ // tq),
        in_specs=[pl.BlockSpec((1, HEAD_COLS, tq), lambda b, h, i: (b, h, i)),
                  pl.BlockSpec((1, SEQ, HEAD_COLS), lambda b, h, i: (b, 0, h)),
                  pl.BlockSpec((1, 1, V_ROWS, SEQ), lambda b, h, i: (b, h, 0, 0)),
                  pl.BlockSpec((1, 2, ATTN_TK, tq), lambda b, h, i: (h, 0, 0, 0)),
                  pl.BlockSpec((1, HEAD_DIM), small), pl.BlockSpec((1, HEAD_DIM), small),
                  pl.BlockSpec((1, HEAD_DIM), small), pl.BlockSpec((1, HEAD_DIM), small),
                  pl.BlockSpec((HEAD_COLS, 1), small)],
        out_specs=pl.BlockSpec((1, tq, HEAD_COLS), lambda b, h, i: (b, i, h)),
        scratch_shapes=[pltpu.VMEM((2, 1, tq), F32),
                        pltpu.VMEM((2, V_ROWS, tq), F32),
                        pltpu.VMEM((2, HEAD_COLS, tq), BF16)],
        compiler_params=pltpu.CompilerParams(vmem_limit_bytes=VMEM_LIMIT_BYTES),
        name="diff_attention",
    )(qt, k, vt, bias_tiles, lam_q1, lam_k1, lam_q2, lam_k2, subln_g)


def _ssm_prep_kernel(are_ref, aim_ref, ldt_ref, ebr_ref, ebi_ref, ecr_ref, eci_ref,
                     eb_ref, ec_ref, abar_ref):
    ar = are_ref[0]
    ai = aim_ref[0]
    dt = jnp.exp(ldt_ref[0])
    mag = jnp.exp(ar * dt)
    ang = ai * dt
    pr = mag * jnp.cos(ang)
    pim = mag * jnp.sin(ang)
    den = ar * ar + ai * ai
    fr = ((pr - 1.0) * ar + pim * ai) / den
    fi = (pim * ar - (pr - 1.0) * ai) / den
    ebr = ebr_ref[0]
    ebi = ebi_ref[0]
    ns = SSM_HALF_STATE
    eb_ref[0, :, 0:ns] = (fr * ebr - fi * ebi).astype(BF16)
    eb_ref[0, :, ns:2 * ns] = (fr * ebi + fi * ebr).astype(BF16)
    ec_ref[0, 0:ns, :] = ecr_ref[0].astype(BF16)
    ec_ref[0, ns:2 * ns, :] = (-eci_ref[0]).astype(BF16)
    qr, qi = pr, pim
    for _ in range(int(math.log2(SSM_CHUNK))):
        qr, qi = qr * qr - qi * qi, 2.0 * qr * qi
    abar_ref[0] = jnp.concatenate([pr, pim, qr, qi, jnp.zeros((4, ns), F32)], axis=0)


def _block_diag_half(w):
    g2 = SSM_GROUPS // 2
    w = w.reshape(2, g2, w.shape[1], w.shape[2])
    eye = jnp.eye(g2, dtype=w.dtype)
    out = w[:, :, :, None, :] * eye[None, :, None, :, None]
    return out.reshape(2, g2 * w.shape[2], g2 * w.shape[3])


def _ssm_prep(a_re, a_im, b_re, b_im, c_re, c_im, log_dt):
    ns = SSM_HALF_STATE
    row = lambda v: v.reshape(2, 1, ns)
    ldt = jnp.broadcast_to(log_dt[:, None], (SSM_GROUPS, SSM_STATE))
    ebr = _block_diag_half(jnp.transpose(b_re, (0, 2, 1)))
    ebi = _block_diag_half(jnp.transpose(b_im, (0, 2, 1)))
    ecr = _block_diag_half(jnp.transpose(c_re, (0, 2, 1)))
    eci = _block_diag_half(jnp.transpose(c_im, (0, 2, 1)))
    vec = pl.BlockSpec((1, 1, ns), lambda j: (j, 0, 0))
    return pl.pallas_call(
        _ssm_prep_kernel,
        out_shape=(jax.ShapeDtypeStruct((2, SSM_HALF, 2 * ns), BF16),
                   jax.ShapeDtypeStruct((2, 2 * ns, SSM_HALF), BF16),
                   jax.ShapeDtypeStruct((2, 8, ns), F32)),
        grid=(2,),
        in_specs=[vec, vec, vec,
                  pl.BlockSpec((1, SSM_HALF, ns), lambda j: (j, 0, 0)),
                  pl.BlockSpec((1, SSM_HALF, ns), lambda j: (j, 0, 0)),
                  pl.BlockSpec((1, ns, SSM_HALF), lambda j: (j, 0, 0)),
                  pl.BlockSpec((1, ns, SSM_HALF), lambda j: (j, 0, 0))],
        out_specs=(pl.BlockSpec((1, SSM_HALF, 2 * ns), lambda j: (j, 0, 0)),
                   pl.BlockSpec((1, 2 * ns, SSM_HALF), lambda j: (j, 0, 0)),
                   pl.BlockSpec((1, 8, ns), lambda j: (j, 0, 0))),
        name="ssm_discretise",
    )(row(a_re), row(a_im), row(ldt), ebr, ebi, ecr, eci)


def _ssm_kernel(s0_ref, s1_ref, s2_ref, s3_ref, eb_ref, ec_ref, abar_ref, d_ref, wglu_ref, bglu_ref,
                y0_ref, y1_ref, y2_ref, y3_ref, h_ref, carry_ref):
    nb, nr, nt, ns = BATCH, SSM_ROWS, SSM_CHUNK, SSM_HALF_STATE
    rows = nb * nr
    s_refs = (s0_ref, s1_ref, s2_ref, s3_ref)
    y_refs = (y0_ref, y1_ref, y2_ref, y3_ref)

    @pl.when(pl.program_id(0) == 0)
    def _():
        carry_ref[...] = jnp.zeros_like(carry_ref)

    def s_tile(t, lo, width):
        step = pl.ds(t, nr, stride=nt)
        slabs = [s_refs[k][:, step, :].reshape(rows, LANES)
                 for k in range(lo // LANES, (lo + width) // LANES)]
        return jnp.concatenate(slabs, axis=1)

    ntile = ns // LANES

    def advance(j, t):
        bu = jnp.dot(s_tile(t, j * SSM_HALF, SSM_HALF).astype(BF16), eb_ref[j],
                     preferred_element_type=F32)
        tiles = [None] * (2 * ntile)
        for k in range(ntile):
            lo = k * LANES
            ar = abar_ref[j, 0:1, lo:lo + LANES]
            ai = abar_ref[j, 1:2, lo:lo + LANES]
            hr = h_ref[j, k]
            hi = h_ref[j, ntile + k]
            new_r = ar * hr - ai * hi + bu[:, lo:lo + LANES]
            new_i = ar * hi + ai * hr + bu[:, ns + lo:ns + lo + LANES]
            h_ref[j, k] = new_r
            h_ref[j, ntile + k] = new_i
            tiles[k], tiles[ntile + k] = new_r, new_i
        return tiles

    h_ref[...] = jnp.zeros_like(h_ref)

    def local_step(t, carry):
        for j in range(2):
            advance(j, t)
        return carry

    lax.fori_loop(0, nt, local_step, 0)

    for j in range(2):
        for k in range(ntile):
            lo = k * LANES
            qr = abar_ref[j, 2:3, lo:lo + LANES]
            qi = abar_ref[j, 3:4, lo:lo + LANES]
            st_r = carry_ref[j, k]
            st_i = carry_ref[j, ntile + k]
            for c in range(nr):
                sel = pl.ds(c, nb, stride=nr)
                loc_r = h_ref[j, k, sel, :]
                loc_i = h_ref[j, ntile + k, sel, :]
                h_ref[j, k, sel, :] = st_r
                h_ref[j, ntile + k, sel, :] = st_i
                st_r, st_i = qr * st_r - qi * st_i + loc_r, qr * st_i + qi * st_r + loc_i
            carry_ref[j, k] = st_r
            carry_ref[j, ntile + k] = st_i

    def output_step(t, carry):
        ys = []
        for j in range(2):
            hcat = jnp.concatenate(advance(j, t), axis=1).astype(BF16)
            ys.append(jnp.dot(hcat, ec_ref[j], preferred_element_type=F32))
        y = jnp.concatenate(ys, axis=1) + d_ref[...] * s_tile(t, 0, SSM_WIDTH)
        y = jax.nn.gelu(y)
        z = jnp.dot(y.astype(BF16), wglu_ref[...], preferred_element_type=F32) + bglu_ref[...]
        out = y * jax.nn.sigmoid(z)
        step = pl.ds(t, nr, stride=nt)
        for k in range(SSM_WIDTH // LANES):
            y_refs[k][:, step, :] = out[:, k * LANES:(k + 1) * LANES].reshape(nb, nr, LANES)
        return carry

    lax.fori_loop(0, nt, output_step, 0)


def _ssm(s, eb, ec, abar, d_skip, w_glu_b, b_glu):
    nt, nr, ns = SSM_CHUNK, SSM_ROWS, SSM_HALF_STATE
    nslab = SSM_WIDTH // LANES
    const3 = lambda i: (0, 0, 0)
    const2 = lambda i: (0, 0)
    once = dict(pipeline_mode=pl.Buffered(1))
    slab = lambda k: pl.BlockSpec((BATCH, nr * nt, LANES), lambda i: (0, i, k))
    return pl.pallas_call(
        _ssm_kernel,
        out_shape=tuple(jax.ShapeDtypeStruct((BATCH, SEQ, LANES), F32) for _ in range(nslab)),
        grid=(SEQ // (nr * nt),),
        in_specs=[slab(k) for k in range(nslab)] + [
            pl.BlockSpec((2, SSM_HALF, 2 * ns), const3, **once),
            pl.BlockSpec((2, 2 * ns, SSM_HALF), const3, **once),
            pl.BlockSpec((2, 8, ns), const3),
            pl.BlockSpec((1, SSM_WIDTH), const2),
            pl.BlockSpec((SSM_WIDTH, SSM_WIDTH), const2),
            pl.BlockSpec((1, SSM_WIDTH), const2)],
        out_specs=tuple(slab(0) for _ in range(nslab)),
        scratch_shapes=[pltpu.VMEM((2, 2 * ns // LANES, BATCH * nr, LANES), F32),
                        pltpu.VMEM((2, 2 * ns // LANES, BATCH, LANES), F32)],
        compiler_params=pltpu.CompilerParams(dimension_semantics=("arbitrary",),
                                             vmem_limit_bytes=VMEM_LIMIT_BYTES),
        name="s5_ssm_glu",
    )(s, s, s, s, eb, ec, abar, d_skip.reshape(1, SSM_WIDTH), w_glu_b, b_glu.reshape(1, SSM_WIDTH))


def _ffn_kernel(x_ref, o_ref, y0_ref, y1_ref, y2_ref, y3_ref, g1_ref, sh2_ref, sc2_ref, g2_ref, wo_ref,
                ln1g_ref, ln1b_ref, wup_ref, cw_ref, wdn_ref, ln2g_ref, ln2b_ref, out_ref, carry_ref):
    tm = FFN_ROWS

    @pl.when(pl.program_id(1) == 0)
    def _():
        carry_ref[...] = jnp.zeros_like(carry_ref)

    y = jnp.concatenate([y0_ref[0], y1_ref[0], y2_ref[0], y3_ref[0]], axis=1).astype(BF16)
    m = (jnp.dot(o_ref[0], wo_ref[0:ATTN_WIDTH, :], preferred_element_type=F32)
         + jnp.dot(y, wo_ref[ATTN_WIDTH:, :], preferred_element_type=F32))
    x1 = _layernorm(DEEPNORM_ALPHA * x_ref[0] + g1_ref[0] * m) * ln1g_ref[...] + ln1b_ref[...]
    ub = (_layernorm(x1) * (1.0 + sc2_ref[0]) + sh2_ref[0]).astype(BF16)

    halo = [carry_ref[idx] for idx in range(2 * FFN_STEPS)]
    tails = [None] * (2 * FFN_STEPS)

    def conv_cols(idx):
        hcur = jnp.dot(ub, wup_ref[idx], preferred_element_type=F32)
        ext = jnp.concatenate([halo[idx], hcur], axis=0)
        tails[idx] = hcur[tm - SUBLANES:tm, :]
        w = cw_ref[idx]
        return (w[2:3] * hcur + w[1:2] * ext[SUBLANES - 1:SUBLANES - 1 + tm, :]
                + w[0:1] * ext[SUBLANES - 2:SUBLANES - 2 + tm, :] + w[3:4])

    f = None
    for j in range(FFN_STEPS):
        val = conv_cols(j)
        gate = conv_cols(j + FFN_STEPS)
        a = (gate * jax.nn.sigmoid(gate) * val).astype(BF16)
        part = jnp.dot(a, wdn_ref[j], preferred_element_type=F32)
        f = part if f is None else f + part
    for idx in range(2 * FFN_STEPS):
        carry_ref[idx] = tails[idx]
    out_ref[0] = (_layernorm(DEEPNORM_ALPHA * x1 + g2_ref[0] * f) * ln2g_ref[...] + ln2b_ref[...])


def _outproj_ffn(x, o, y_slabs, mod3, w_out_b, ln1_g, ln1_b, w_up_c, conv_c, w_down_c, ln2_g, ln2_b):
    tm = FFN_ROWS
    row = lambda b, i: (b, i, 0)
    modv = lambda k: pl.BlockSpec((1, 1, D_MODEL), lambda b, i: (b, 0, k))
    c2 = lambda b, i: (0, 0)
    c3 = lambda b, i: (0, 0, 0)
    vec = pl.BlockSpec((1, D_MODEL), c2)
    return pl.pallas_call(
        _ffn_kernel,
        out_shape=jax.ShapeDtypeStruct((BATCH, SEQ, D_MODEL), F32),
        grid=(BATCH, SEQ // tm),
        in_specs=[pl.BlockSpec((1, tm, D_MODEL), row),
                  pl.BlockSpec((1, tm, ATTN_WIDTH), row),
                  *[pl.BlockSpec((1, tm, LANES), row) for _ in range(SSM_WIDTH // LANES)],
                  modv(2), modv(3), modv(4), modv(5),
                  pl.BlockSpec((D_MODEL, D_MODEL), c2, pipeline_mode=pl.Buffered(1)),
                  vec, vec,
                  pl.BlockSpec((2 * FFN_STEPS, D_MODEL, FFN_COLS), c3, pipeline_mode=pl.Buffered(1)),
                  pl.BlockSpec((2 * FFN_STEPS, SUBLANES, FFN_COLS), c3),
                  pl.BlockSpec((FFN_STEPS, FFN_COLS, D_MODEL), c3, pipeline_mode=pl.Buffered(1)),
                  vec, vec],
        out_specs=pl.BlockSpec((1, tm, D_MODEL), row),
        scratch_shapes=[pltpu.VMEM((2 * FFN_STEPS, SUBLANES, FFN_COLS), F32)],
        compiler_params=pltpu.CompilerParams(dimension_semantics=("arbitrary", "arbitrary"),
                                             vmem_limit_bytes=VMEM_LIMIT_BYTES),
        name="outproj_ffn",
    )(x, o, *y_slabs, mod3, mod3, mod3, mod3, w_out_b, ln1_g, ln1_b, w_up_c, conv_c, w_down_c, ln2_g, ln2_b)


def kernel(x, c, w_ada, b_ada, w_in, lam_q1, lam_k1, lam_q2, lam_k2, subln_g, ssm_a_re, ssm_a_im,
           ssm_b_re, ssm_b_im, ssm_c_re, ssm_c_im, ssm_d, ssm_log_dt, w_glu, b_glu, w_out, ln1_g,
           ln1_b, w_up, conv_w, conv_b, w_down, ln2_g, ln2_b, rel_bias):
    bias_tiles = _bias_tiles(rel_bias)
    for layer in range(DEPTH):
        lam_init = 0.8 - 0.6 * math.exp(-0.3 * layer)
        mod3 = _modulation(c, w_ada[layer], b_ada[layer]).reshape(BATCH, 1, 6 * D_MODEL)
        qt, k, vt, s = _in_projection(x, mod3, w_in[layer].astype(BF16))
        o = _attention(qt, k, vt, bias_tiles, lam_q1[layer][None], lam_k1[layer][None],
                       lam_q2[layer][None], lam_k2[layer][None], subln_g[layer][:, None], lam_init)
        eb, ec, abar = _ssm_prep(ssm_a_re[layer], ssm_a_im[layer], ssm_b_re[layer], ssm_b_im[layer],
                                 ssm_c_re[layer], ssm_c_im[layer], ssm_log_dt[layer])
        y = _ssm(s, eb, ec, abar, ssm_d[layer], w_glu[layer].astype(BF16), b_glu[layer])
        w_up_c = jnp.transpose(w_up[layer].astype(BF16).reshape(D_MODEL, 2 * FFN_STEPS, FFN_COLS),
                               (1, 0, 2))
        conv_c = jnp.concatenate([conv_w[layer], conv_b[layer][None],
                                  jnp.zeros((SUBLANES - 4, 2 * D_FF), F32)], axis=0)
        conv_c = jnp.transpose(conv_c.reshape(SUBLANES, 2 * FFN_STEPS, FFN_COLS), (1, 0, 2))
        w_down_c = w_down[layer].astype(BF16).reshape(FFN_STEPS, FFN_COLS, D_MODEL)
        x = _outproj_ffn(x, o, y, mod3, w_out[layer].astype(BF16), ln1_g[layer][None],
                         ln1_b[layer][None], w_up_c, conv_c, w_down_c, ln2_g[layer][None],
                         ln2_b[layer][None])
    return x
```

```python
import functools
import math

import numpy as np
import jax
import jax.numpy as jnp
from jax import lax
from jax.experimental import pallas as pl
from jax.experimental.pallas import tpu as pltpu

F32 = jnp.float32
BF16 = jnp.bfloat16

D_MODEL = 1024
BATCH = 16
SEQ = 4096
ATTN_WIDTH = 512
SSM_WIDTH = 512
ATTN_HEADS = 4
HEAD_DIM = 64
HEAD_COLS = 2 * HEAD_DIM
SSM_GROUP = 16
SSM_GROUPS = 32
SSM_STATE = 64
D_FF = 2816
REL_BUCKETS = 32
REL_MAX_DIST = 128
LN_EPS = 1e-5
NEG_INF = -1e30
DEPTH = 1
DEEPNORM_ALPHA = (2 * DEPTH) ** 0.25

LANES = 128
SUBLANES = 8
MXU_DIM = 256
VMEM_LIMIT_BYTES = 56 * 1024 * 1024

PROJ_ROWS = 512
ATTN_TQ = 512
ATTN_TK = 512
V_ROWS = HEAD_COLS + 16
LOG2E = math.log2(math.e)
SSM_CHUNK = 16
SSM_ROWS = 16
SSM_HALF = SSM_WIDTH // 2
SSM_HALF_STATE = (SSM_GROUPS // 2) * SSM_STATE
FFN_ROWS = 512
FFN_COLS = 256
FFN_STEPS = D_FF // FFN_COLS
FFN_GROUP = 4


def _layernorm(x):
    mu = jnp.mean(x, axis=-1, keepdims=True)
    xc = x - mu
    var = jnp.mean(xc * xc, axis=-1, keepdims=True)
    return xc * lax.rsqrt(var + LN_EPS)


def _mod_kernel(c_ref, w_ref, b_ref, o_ref):
    c = c_ref[...]
    act = c * jax.nn.sigmoid(c)
    o_ref[...] = jnp.dot(act, w_ref[...], precision=lax.Precision.HIGHEST,
                         preferred_element_type=F32) + b_ref[...]


def _modulation(c, w_ada, b_ada):
    n = w_ada.shape[1]
    blk = D_MODEL
    return pl.pallas_call(
        _mod_kernel,
        out_shape=jax.ShapeDtypeStruct((BATCH, n), F32),
        grid=(n // blk,),
        in_specs=[pl.BlockSpec((BATCH, D_MODEL), lambda i: (0, 0)),
                  pl.BlockSpec((D_MODEL, blk), lambda i: (0, i)),
                  pl.BlockSpec((1, blk), lambda i: (0, i))],
        out_specs=pl.BlockSpec((BATCH, blk), lambda i: (0, i)),
        name="adaln_mod",
    )(c, w_ada, b_ada.reshape(1, n))


def _inproj_kernel(x_ref, sh_ref, sc_ref, wqt_ref, wk_ref, wvt_ref, ws_ref,
                   qt_ref, k_ref, vt_ref, s_ref):
    u = _layernorm(x_ref[0]) * (1.0 + sc_ref[0]) + sh_ref[0]
    ub = u.astype(BF16)
    nt = (((1,), (1,)), ((), ()))
    qt = lax.dot_general(wqt_ref[...], ub, nt, preferred_element_type=F32)
    qt_ref[0] = (qt * (HEAD_DIM ** -0.5 * LOG2E)).astype(BF16)
    k_ref[0] = jnp.dot(ub, wk_ref[...], preferred_element_type=F32).astype(BF16)
    vt = lax.dot_general(wvt_ref[...], ub, nt, preferred_element_type=F32).astype(BF16)
    ones = jnp.ones((V_ROWS - HEAD_COLS, vt.shape[1]), BF16)
    for h in range(ATTN_HEADS):
        vt_ref[0, h, 0:HEAD_COLS, :] = vt[h * HEAD_COLS:(h + 1) * HEAD_COLS, :]
        vt_ref[0, h, HEAD_COLS:V_ROWS, :] = ones
    s_ref[0] = jnp.dot(ub, ws_ref[...], preferred_element_type=F32)


def _in_projection(x, mod3, w_in_b):
    tm = PROJ_ROWS
    aw = ATTN_WIDTH
    wqt = jnp.transpose(w_in_b[:, 0:aw])
    wk = w_in_b[:, aw:2 * aw]
    wvt = jnp.transpose(w_in_b[:, 2 * aw:3 * aw])
    ws = w_in_b[:, 3 * aw:]
    const = lambda b, i: (0, 0)
    tposed = pl.BlockSpec((1, aw, tm), lambda b, i: (b, 0, i))
    rowblk = lambda w: pl.BlockSpec((1, tm, w), lambda b, i: (b, i, 0))
    return pl.pallas_call(
        _inproj_kernel,
        out_shape=(jax.ShapeDtypeStruct((BATCH, aw, SEQ), BF16),
                   jax.ShapeDtypeStruct((BATCH, SEQ, aw), BF16),
                   jax.ShapeDtypeStruct((BATCH, ATTN_HEADS, V_ROWS, SEQ), BF16),
                   jax.ShapeDtypeStruct((BATCH, SEQ, SSM_WIDTH), F32)),
        grid=(BATCH, SEQ // tm),
        in_specs=[rowblk(D_MODEL),
                  pl.BlockSpec((1, 1, D_MODEL), lambda b, i: (b, 0, 0)),
                  pl.BlockSpec((1, 1, D_MODEL), lambda b, i: (b, 0, 1)),
                  pl.BlockSpec((aw, D_MODEL), const), pl.BlockSpec((D_MODEL, aw), const),
                  pl.BlockSpec((aw, D_MODEL), const), pl.BlockSpec((D_MODEL, SSM_WIDTH), const)],
        out_specs=(tposed, rowblk(aw),
                   pl.BlockSpec((1, ATTN_HEADS, V_ROWS, tm), lambda b, i: (b, 0, 0, i)),
                   rowblk(SSM_WIDTH)),
        compiler_params=pltpu.CompilerParams(vmem_limit_bytes=VMEM_LIMIT_BYTES),
        name="ln_inproj",
    )(x, mod3, mod3, wqt, wk, wvt, ws)


def _t5_bucket_np(dist):
    max_exact = REL_BUCKETS // 2
    distf = np.maximum(dist, 1).astype(np.float32)
    large = max_exact + (np.log(distf / max_exact) / math.log(REL_MAX_DIST / max_exact)
                         * (REL_BUCKETS - max_exact)).astype(np.int32)
    large = np.minimum(large, REL_BUCKETS - 1)
    return np.where(dist < max_exact, dist, large).astype(np.int32)


def _bucket_tiles():
    kpos = np.arange(ATTN_TK)[:, None]
    qpos = np.arange(ATTN_TQ)[None, :]
    diag = qpos - kpos
    diag_b = np.where(diag >= 0, _t5_bucket_np(np.maximum(diag, 0)), -1)
    sub_b = _t5_bucket_np(diag + ATTN_TK)
    return np.stack([diag_b, sub_b]).astype(np.int32)


def _bias_kernel(tab_ref, bucket_ref, o_ref):
    h = pl.program_id(0)
    far = tab_ref[REL_BUCKETS - 1, h]
    for tile in range(2):
        b = bucket_ref[tile]
        acc = jnp.full(b.shape, NEG_INF, F32)
        for n in range(REL_BUCKETS):
            acc = jnp.where(b == n, (tab_ref[n, h] - far) * LOG2E, acc)
        o_ref[0, tile] = acc


def _bias_tiles(rel_bias):
    buckets = jnp.asarray(_bucket_tiles())
    return pl.pallas_call(
        _bias_kernel,
        out_shape=jax.ShapeDtypeStruct((ATTN_HEADS, 2, ATTN_TK, ATTN_TQ), F32),
        grid=(ATTN_HEADS,),
        in_specs=[pl.BlockSpec(memory_space=pltpu.SMEM),
                  pl.BlockSpec((2, ATTN_TK, ATTN_TQ), lambda h: (0, 0, 0))],
        out_specs=pl.BlockSpec((1, 2, ATTN_TK, ATTN_TQ), lambda h: (h, 0, 0, 0)),
        name="t5_bias_tiles",
    )(rel_bias, buckets)


def _attn_kernel(qt_ref, k_ref, vt_ref, bias_ref, lq1_ref, lk1_ref, lq2_ref, lk2_ref, g_ref,
                 o_ref, m_ref, acc_ref, qs_ref, *, lam_init):
    qi = pl.program_id(2)
    tq, tk = ATTN_TQ, ATTN_TK

    qt = qt_ref[0].astype(F32)
    row = lax.broadcasted_iota(jnp.int32, qt.shape, 0)
    qs_ref[0] = jnp.where(row < HEAD_DIM, qt, 0.0).astype(BF16)
    qs_ref[1] = jnp.where(row >= HEAD_DIM, qt, 0.0).astype(BF16)

    def scores(j):
        k = k_ref[0, pl.ds(pl.multiple_of(j * tk, tk), tk), :]
        return [jnp.dot(k, qs_ref[c], preferred_element_type=F32) for c in range(2)]

    def update(j, ss):
        vt = vt_ref[0, 0, :, pl.ds(pl.multiple_of(j * tk, tk), tk)]
        for c in range(2):
            s = ss[c]
            m_prev = m_ref[c]
            m_new = jnp.maximum(m_prev, jnp.max(s, axis=0, keepdims=True))
            alpha = jnp.exp2(m_prev - m_new)
            p = jnp.exp2(s - m_new).astype(BF16)
            acc_ref[c] = alpha * acc_ref[c] + jnp.dot(vt, p, preferred_element_type=F32)
            m_ref[c] = m_new

    m_ref[...] = jnp.full(m_ref.shape, NEG_INF, F32)
    acc_ref[...] = jnp.zeros_like(acc_ref)

    nfar = jnp.maximum(qi - 1, 0)

    def far_pair(i, carry):
        s_a = scores(2 * i)
        s_b = scores(2 * i + 1)
        update(2 * i, s_a)
        update(2 * i + 1, s_b)
        return carry

    lax.fori_loop(0, nfar // 2, far_pair, 0)

    def far_single(i, carry):
        update(nfar - 1, scores(nfar - 1))
        return carry

    lax.fori_loop(0, nfar % 2, far_single, 0)

    half = tk // 2

    def diag_scores():
        parts = []
        for part in range(2):
            koff = pl.multiple_of(qi * tk + part * half, half)
            k = k_ref[0, pl.ds(koff, half), :]
            q0 = part * half
            parts.append([jnp.dot(k, qs_ref[c, :, q0:tq], preferred_element_type=F32)
                          + bias_ref[0, 0, part * half:(part + 1) * half, q0:tq] for c in range(2)])
        return parts

    def diag_update(parts):
        for part in range(2):
            koff = pl.multiple_of(qi * tk + part * half, half)
            vt = vt_ref[0, 0, :, pl.ds(koff, half)]
            q0 = part * half
            for c in range(2):
                s = parts[part][c]
                m_prev = m_ref[c, :, q0:tq]
                m_new = jnp.maximum(m_prev, jnp.max(s, axis=0, keepdims=True))
                alpha = jnp.exp2(m_prev - m_new)
                p = jnp.exp2(s - m_new).astype(BF16)
                acc_ref[c, :, q0:tq] = (alpha * acc_ref[c, :, q0:tq]
                                        + jnp.dot(vt, p, preferred_element_type=F32))
                m_ref[c, :, q0:tq] = m_new

    def sub_and_diag(i, carry):
        s_sub = [s + bias_ref[0, 1] for s in scores(qi - 1)]
        parts = diag_scores()
        update(qi - 1, s_sub)
        diag_update(parts)
        return carry

    lax.fori_loop(0, jnp.minimum(qi, 1), sub_and_diag, 0)

    def diag_only(i, carry):
        diag_update(diag_scores())
        return carry

    lax.fori_loop(0, 1 - jnp.minimum(qi, 1), diag_only, 0)

    lam = (jnp.exp(jnp.sum(lq1_ref[...] * lk1_ref[...], axis=1, keepdims=True))
           - jnp.exp(jnp.sum(lq2_ref[...] * lk2_ref[...], axis=1, keepdims=True)) + lam_init)
    on = [acc_ref[c, 0:HEAD_COLS, :] / acc_ref[c, HEAD_COLS:HEAD_COLS + 1, :] for c in range(2)]
    ot = on[0] - lam * on[1]
    ms = jnp.mean(ot * ot, axis=0, keepdims=True)
    ot = ot * lax.rsqrt(ms + LN_EPS) * g_ref[...] * (1.0 - lam_init)
    o_ref[0] = jnp.transpose(ot).astype(BF16)


def _attention(qt, k, vt, bias_tiles, lam_q1, lam_k1, lam_q2, lam_k2, subln_g, lam_init):
    tq = ATTN_TQ
    small = lambda b, h, i: (0, 0)
    return pl.pallas_call(
        functools.partial(_attn_kernel, lam_init=lam_init),
        out_shape=jax.ShapeDtypeStruct((BATCH, SEQ, ATTN_WIDTH), BF16),
        grid=(BATCH, ATTN_HEADS, SEQ // tq),
        in_specs=[pl.BlockSpec((1, HEAD_COLS, tq), lambda b, h, i: (b, h, i)),
                  pl.BlockSpec((1, SEQ, HEAD_COLS), lambda b, h, i: (b, 0, h)),
                  pl.BlockSpec((1, 1, V_ROWS, SEQ), lambda b, h, i: (b, h, 0, 0)),
                  pl.BlockSpec((1, 2, ATTN_TK, tq), lambda b, h, i: (h, 0, 0, 0)),
                  pl.BlockSpec((1, HEAD_DIM), small), pl.BlockSpec((1, HEAD_DIM), small),
                  pl.BlockSpec((1, HEAD_DIM), small), pl.BlockSpec((1, HEAD_DIM), small),
                  pl.BlockSpec((HEAD_COLS, 1), small)],
        out_specs=pl.BlockSpec((1, tq, HEAD_COLS), lambda b, h, i: (b, i, h)),
        scratch_shapes=[pltpu.VMEM((2, 1, tq), F32),
                        pltpu.VMEM((2, V_ROWS, tq), F32),
                        pltpu.VMEM((2, HEAD_COLS, tq), BF16)],
        compiler_params=pltpu.CompilerParams(vmem_limit_bytes=VMEM_LIMIT_BYTES),
        name="diff_attention",
    )(qt, k, vt, bias_tiles, lam_q1, lam_k1, lam_q2, lam_k2, subln_g)


def _ssm_prep_kernel(are_ref, aim_ref, ldt_ref, ebr_ref, ebi_ref, ecr_ref, eci_ref,
                     eb_ref, ec_ref, abar_ref):
    ar = are_ref[0]
    ai = aim_ref[0]
    dt = jnp.exp(ldt_ref[0])
    mag = jnp.exp(ar * dt)
    ang = ai * dt
    pr = mag * jnp.cos(ang)
    pim = mag * jnp.sin(ang)
    den = ar * ar + ai * ai
    fr = ((pr - 1.0) * ar + pim * ai) / den
    fi = (pim * ar - (pr - 1.0) * ai) / den
    ebr = ebr_ref[0]
    ebi = ebi_ref[0]
    ns = SSM_HALF_STATE
    bre = (fr * ebr - fi * ebi).astype(BF16)
    bim = (fr * ebi + fi * ebr).astype(BF16)
    cre = ecr_ref[0].astype(BF16)
    cim = (-eci_ref[0]).astype(BF16)
    for k in range(ns // LANES):
        src = slice(k * LANES, (k + 1) * LANES)
        eb_ref[0, :, 2 * k * LANES:(2 * k + 1) * LANES] = bre[:, src]
        eb_ref[0, :, (2 * k + 1) * LANES:(2 * k + 2) * LANES] = bim[:, src]
        ec_ref[0, 2 * k * LANES:(2 * k + 1) * LANES, :] = cre[src, :]
        ec_ref[0, (2 * k + 1) * LANES:(2 * k + 2) * LANES, :] = cim[src, :]
    qr, qi = pr, pim
    for _ in range(int(math.log2(SSM_CHUNK))):
        qr, qi = qr * qr - qi * qi, 2.0 * qr * qi
    abar_ref[0] = jnp.concatenate([pr, pim, qr, qi, jnp.zeros((4, ns), F32)], axis=0)


def _block_diag_half(w):
    g2 = SSM_GROUPS // 2
    w = w.reshape(2, g2, w.shape[1], w.shape[2])
    eye = jnp.eye(g2, dtype=w.dtype)
    out = w[:, :, :, None, :] * eye[None, :, None, :, None]
    return out.reshape(2, g2 * w.shape[2], g2 * w.shape[3])


def _ssm_prep(a_re, a_im, b_re, b_im, c_re, c_im, log_dt):
    ns = SSM_HALF_STATE
    row = lambda v: v.reshape(2, 1, ns)
    ldt = jnp.broadcast_to(log_dt[:, None], (SSM_GROUPS, SSM_STATE))
    ebr = _block_diag_half(jnp.transpose(b_re, (0, 2, 1)))
    ebi = _block_diag_half(jnp.transpose(b_im, (0, 2, 1)))
    ecr = _block_diag_half(jnp.transpose(c_re, (0, 2, 1)))
    eci = _block_diag_half(jnp.transpose(c_im, (0, 2, 1)))
    vec = pl.BlockSpec((1, 1, ns), lambda j: (j, 0, 0))
    return pl.pallas_call(
        _ssm_prep_kernel,
        out_shape=(jax.ShapeDtypeStruct((2, SSM_HALF, 2 * ns), BF16),
                   jax.ShapeDtypeStruct((2, 2 * ns, SSM_HALF), BF16),
                   jax.ShapeDtypeStruct((2, 8, ns), F32)),
        grid=(2,),
        in_specs=[vec, vec, vec,
                  pl.BlockSpec((1, SSM_HALF, ns), lambda j: (j, 0, 0)),
                  pl.BlockSpec((1, SSM_HALF, ns), lambda j: (j, 0, 0)),
                  pl.BlockSpec((1, ns, SSM_HALF), lambda j: (j, 0, 0)),
                  pl.BlockSpec((1, ns, SSM_HALF), lambda j: (j, 0, 0))],
        out_specs=(pl.BlockSpec((1, SSM_HALF, 2 * ns), lambda j: (j, 0, 0)),
                   pl.BlockSpec((1, 2 * ns, SSM_HALF), lambda j: (j, 0, 0)),
                   pl.BlockSpec((1, 8, ns), lambda j: (j, 0, 0))),
        name="ssm_discretise",
    )(row(a_re), row(a_im), row(ldt), ebr, ebi, ecr, eci)


def _ssm_kernel(s0_ref, s1_ref, s2_ref, s3_ref, eb_ref, ec_ref, abar_ref, d_ref, wglu_ref, bglu_ref,
                y0_ref, y1_ref, y2_ref, y3_ref, h_ref, carry_ref):
    nb, nr, nt, ns = BATCH, SSM_ROWS, SSM_CHUNK, SSM_HALF_STATE
    rows = nb * nr
    s_refs = (s0_ref, s1_ref, s2_ref, s3_ref)
    y_refs = (y0_ref, y1_ref, y2_ref, y3_ref)

    @pl.when(pl.program_id(0) == 0)
    def _():
        carry_ref[...] = jnp.zeros_like(carry_ref)

    def s_tile(t, lo, width):
        step = pl.ds(t, nr, stride=nt)
        slabs = [s_refs[k][:, step, :].reshape(rows, LANES)
                 for k in range(lo // LANES, (lo + width) // LANES)]
        return jnp.concatenate(slabs, axis=1)

    ntile = ns // LANES

    strip = 2 * LANES

    def advance(j, s_j):
        tiles = [None] * (2 * ntile)
        for k in range(ntile):
            lo = k * LANES
            bu = jnp.dot(s_j, eb_ref[j, :, k * strip:(k + 1) * strip], preferred_element_type=F32)
            ar = abar_ref[j, 0:1, lo:lo + LANES]
            ai = abar_ref[j, 1:2, lo:lo + LANES]
            hr = h_ref[j, 2 * k]
            hi = h_ref[j, 2 * k + 1]
            tiles[2 * k] = ar * hr - ai * hi + bu[:, 0:LANES]
            tiles[2 * k + 1] = ar * hi + ai * hr + bu[:, LANES:strip]
            h_ref[j, 2 * k] = tiles[2 * k]
            h_ref[j, 2 * k + 1] = tiles[2 * k + 1]
        return tiles

    h_ref[...] = jnp.zeros_like(h_ref)

    def local_step(t, carry):
        for j in range(2):
            advance(j, s_tile(t, j * SSM_HALF, SSM_HALF).astype(BF16))
        return carry

    lax.fori_loop(0, nt, local_step, 0)

    for j in range(2):
        for k in range(ntile):
            lo = k * LANES
            qr = abar_ref[j, 2:3, lo:lo + LANES]
            qi = abar_ref[j, 3:4, lo:lo + LANES]
            st_r = carry_ref[j, 2 * k]
            st_i = carry_ref[j, 2 * k + 1]
            for c in range(nr):
                sel = pl.ds(c, nb, stride=nr)
                loc_r = h_ref[j, 2 * k, sel, :]
                loc_i = h_ref[j, 2 * k + 1, sel, :]
                h_ref[j, 2 * k, sel, :] = st_r
                h_ref[j, 2 * k + 1, sel, :] = st_i
                st_r, st_i = qr * st_r - qi * st_i + loc_r, qr * st_i + qi * st_r + loc_i
            carry_ref[j, 2 * k] = st_r
            carry_ref[j, 2 * k + 1] = st_i

    def output_step(t, carry):
        s_t = s_tile(t, 0, SSM_WIDTH)
        ys = []
        for j in range(2):
            tiles = advance(j, s_t[:, j * SSM_HALF:(j + 1) * SSM_HALF].astype(BF16))
            ys.append(jnp.dot(jnp.concatenate(tiles, axis=1).astype(BF16), ec_ref[j],
                              preferred_element_type=F32))
        y = jnp.concatenate(ys, axis=1) + d_ref[...] * s_t
        y = jax.nn.gelu(y)
        z = jnp.dot(y.astype(BF16), wglu_ref[...], preferred_element_type=F32) + bglu_ref[...]
        out = y * jax.nn.sigmoid(z)
        step = pl.ds(t, nr, stride=nt)
        for k in range(SSM_WIDTH // LANES):
            y_refs[k][:, step, :] = out[:, k * LANES:(k + 1) * LANES].reshape(nb, nr, LANES)
        return carry

    lax.fori_loop(0, nt, output_step, 0)


def _ssm(s, eb, ec, abar, d_skip, w_glu_b, b_glu):
    nt, nr, ns = SSM_CHUNK, SSM_ROWS, SSM_HALF_STATE
    nslab = SSM_WIDTH // LANES
    const3 = lambda i: (0, 0, 0)
    const2 = lambda i: (0, 0)
    once = dict(pipeline_mode=pl.Buffered(1))
    slab = lambda k: pl.BlockSpec((BATCH, nr * nt, LANES), lambda i: (0, i, k))
    return pl.pallas_call(
        _ssm_kernel,
        out_shape=tuple(jax.ShapeDtypeStruct((BATCH, SEQ, LANES), F32) for _ in range(nslab)),
        grid=(SEQ // (nr * nt),),
        in_specs=[slab(k) for k in range(nslab)] + [
            pl.BlockSpec((2, SSM_HALF, 2 * ns), const3, **once),
            pl.BlockSpec((2, 2 * ns, SSM_HALF), const3, **once),
            pl.BlockSpec((2, 8, ns), const3),
            pl.BlockSpec((1, SSM_WIDTH), const2),
            pl.BlockSpec((SSM_WIDTH, SSM_WIDTH), const2),
            pl.BlockSpec((1, SSM_WIDTH), const2)],
        out_specs=tuple(slab(0) for _ in range(nslab)),
        scratch_shapes=[pltpu.VMEM((2, 2 * ns // LANES, BATCH * nr, LANES), F32),
                        pltpu.VMEM((2, 2 * ns // LANES, BATCH, LANES), F32)],
        compiler_params=pltpu.CompilerParams(dimension_semantics=("arbitrary",),
                                             vmem_limit_bytes=VMEM_LIMIT_BYTES),
        name="s5_ssm_glu",
    )(s, s, s, s, eb, ec, abar, d_skip.reshape(1, SSM_WIDTH), w_glu_b, b_glu.reshape(1, SSM_WIDTH))


def _ffn_kernel(x_ref, o_ref, y0_ref, y1_ref, y2_ref, y3_ref, g1_ref, sh2_ref, sc2_ref, g2_ref, wo_ref,
                ln1g_ref, ln1b_ref, wup_ref, cw_ref, wdn_ref, ln2g_ref, ln2b_ref, out_ref, carry_ref):
    tm = FFN_ROWS

    @pl.when(pl.program_id(1) == 0)
    def _():
        carry_ref[...] = jnp.zeros_like(carry_ref)

    y = jnp.concatenate([y0_ref[0], y1_ref[0], y2_ref[0], y3_ref[0]], axis=1).astype(BF16)
    m = (jnp.dot(o_ref[0], wo_ref[0:ATTN_WIDTH, :], preferred_element_type=F32)
         + jnp.dot(y, wo_ref[ATTN_WIDTH:, :], preferred_element_type=F32))
    x1 = _layernorm(DEEPNORM_ALPHA * x_ref[0] + g1_ref[0] * m) * ln1g_ref[...] + ln1b_ref[...]
    ub = (_layernorm(x1) * (1.0 + sc2_ref[0]) + sh2_ref[0]).astype(BF16)

    halo = [carry_ref[idx] for idx in range(2 * FFN_STEPS)]
    tails = [None] * (2 * FFN_STEPS)

    def up(idx):
        return jnp.dot(ub, wup_ref[idx], preferred_element_type=F32)

    def conv_cols(idx, hcur):
        ext = jnp.concatenate([halo[idx], hcur], axis=0)
        tails[idx] = hcur[tm - SUBLANES:tm, :]
        w = cw_ref[idx]
        return (w[2:3] * hcur + w[1:2] * ext[SUBLANES - 1:SUBLANES - 1 + tm, :]
                + w[0:1] * ext[SUBLANES - 2:SUBLANES - 2 + tm, :] + w[3:4])

    f = None
    group = []
    nxt = (up(0), up(FFN_STEPS))
    for j in range(FFN_STEPS):
        cur = nxt
        if j + 1 < FFN_STEPS:
            nxt = (up(j + 1), up(j + 1 + FFN_STEPS))
        val = conv_cols(j, cur[0])
        gate = conv_cols(j + FFN_STEPS, cur[1])
        group.append((gate * jax.nn.sigmoid(gate) * val).astype(BF16))
        if len(group) < FFN_GROUP and j + 1 < FFN_STEPS:
            continue
        lo = (j + 1 - len(group)) * FFN_COLS
        a = jnp.concatenate(group, axis=1) if len(group) > 1 else group[0]
        part = jnp.dot(a, wdn_ref[lo:(j + 1) * FFN_COLS, :], preferred_element_type=F32)
        f = part if f is None else f + part
        group = []
    for idx in range(2 * FFN_STEPS):
        carry_ref[idx] = tails[idx]
    out_ref[0] = (_layernorm(DEEPNORM_ALPHA * x1 + g2_ref[0] * f) * ln2g_ref[...] + ln2b_ref[...])


def _outproj_ffn(x, o, y_slabs, mod3, w_out_b, ln1_g, ln1_b, w_up_c, conv_c, w_down_c, ln2_g, ln2_b):
    tm = FFN_ROWS
    row = lambda b, i: (b, i, 0)
    modv = lambda k: pl.BlockSpec((1, 1, D_MODEL), lambda b, i: (b, 0, k))
    c2 = lambda b, i: (0, 0)
    c3 = lambda b, i: (0, 0, 0)
    vec = pl.BlockSpec((1, D_MODEL), c2)
    return pl.pallas_call(
        _ffn_kernel,
        out_shape=jax.ShapeDtypeStruct((BATCH, SEQ, D_MODEL), F32),
        grid=(BATCH, SEQ // tm),
        in_specs=[pl.BlockSpec((1, tm, D_MODEL), row),
                  pl.BlockSpec((1, tm, ATTN_WIDTH), row),
                  *[pl.BlockSpec((1, tm, LANES), row) for _ in range(SSM_WIDTH // LANES)],
                  modv(2), modv(3), modv(4), modv(5),
                  pl.BlockSpec((D_MODEL, D_MODEL), c2, pipeline_mode=pl.Buffered(1)),
                  vec, vec,
                  pl.BlockSpec((2 * FFN_STEPS, D_MODEL, FFN_COLS), c3, pipeline_mode=pl.Buffered(1)),
                  pl.BlockSpec((2 * FFN_STEPS, SUBLANES, FFN_COLS), c3),
                  pl.BlockSpec((D_FF, D_MODEL), c2, pipeline_mode=pl.Buffered(1)),
                  vec, vec],
        out_specs=pl.BlockSpec((1, tm, D_MODEL), row),
        scratch_shapes=[pltpu.VMEM((2 * FFN_STEPS, SUBLANES, FFN_COLS), F32)],
        compiler_params=pltpu.CompilerParams(dimension_semantics=("arbitrary", "arbitrary"),
                                             vmem_limit_bytes=VMEM_LIMIT_BYTES),
        name="outproj_ffn",
    )(x, o, *y_slabs, mod3, mod3, mod3, mod3, w_out_b, ln1_g, ln1_b, w_up_c, conv_c, w_down_c, ln2_g, ln2_b)


def kernel(x, c, w_ada, b_ada, w_in, lam_q1, lam_k1, lam_q2, lam_k2, subln_g, ssm_a_re, ssm_a_im,
           ssm_b_re, ssm_b_im, ssm_c_re, ssm_c_im, ssm_d, ssm_log_dt, w_glu, b_glu, w_out, ln1_g,
           ln1_b, w_up, conv_w, conv_b, w_down, ln2_g, ln2_b, rel_bias):
    bias_tiles = _bias_tiles(rel_bias)
    for layer in range(DEPTH):
        lam_init = 0.8 - 0.6 * math.exp(-0.3 * layer)
        mod3 = _modulation(c, w_ada[layer], b_ada[layer]).reshape(BATCH, 1, 6 * D_MODEL)
        qt, k, vt, s = _in_projection(x, mod3, w_in[layer].astype(BF16))
        o = _attention(qt, k, vt, bias_tiles, lam_q1[layer][None], lam_k1[layer][None],
                       lam_q2[layer][None], lam_k2[layer][None], subln_g[layer][:, None], lam_init)
        eb, ec, abar = _ssm_prep(ssm_a_re[layer], ssm_a_im[layer], ssm_b_re[layer], ssm_b_im[layer],
                                 ssm_c_re[layer], ssm_c_im[layer], ssm_log_dt[layer])
        y = _ssm(s, eb, ec, abar, ssm_d[layer], w_glu[layer].astype(BF16), b_glu[layer])
        w_up_c = jnp.transpose(w_up[layer].astype(BF16).reshape(D_MODEL, 2 * FFN_STEPS, FFN_COLS),
                               (1, 0, 2))
        conv_c = jnp.concatenate([conv_w[layer], conv_b[layer][None],
                                  jnp.zeros((SUBLANES - 4, 2 * D_FF), F32)], axis=0)
        conv_c = jnp.transpose(conv_c.reshape(SUBLANES, 2 * FFN_STEPS, FFN_COLS), (1, 0, 2))
        x = _outproj_ffn(x, o, y, mod3, w_out[layer].astype(BF16), ln1_g[layer][None],
                         ln1_b[layer][None], w_up_c, conv_c, w_down[layer].astype(BF16),
                         ln2_g[layer][None],
                         ln2_b[layer][None])
    return x
```

```python
import functools
import math

import numpy as np
import jax
import jax.numpy as jnp
from jax import lax
from jax.experimental import pallas as pl
from jax.experimental.pallas import tpu as pltpu

F32 = jnp.float32
BF16 = jnp.bfloat16

D_MODEL = 1024
BATCH = 16
SEQ = 4096
ATTN_WIDTH = 512
SSM_WIDTH = 512
ATTN_HEADS = 4
HEAD_DIM = 64
HEAD_COLS = 2 * HEAD_DIM
SSM_GROUP = 16
SSM_GROUPS = 32
SSM_STATE = 64
D_FF = 2816
REL_BUCKETS = 32
REL_MAX_DIST = 128
LN_EPS = 1e-5
NEG_INF = -1e30
DEPTH = 1
DEEPNORM_ALPHA = (2 * DEPTH) ** 0.25

LANES = 128
SUBLANES = 8
MXU_DIM = 256
VMEM_LIMIT_BYTES = 56 * 1024 * 1024

PROJ_ROWS = 512
ATTN_TQ = 512
ATTN_TK = 512
V_ROWS = HEAD_COLS + 16
LOG2E = math.log2(math.e)
SSM_CHUNK = 16
SSM_ROWS = 16
SSM_BLOCK = 4
SSM_HALF = SSM_WIDTH // 2
SSM_HALF_STATE = (SSM_GROUPS // 2) * SSM_STATE
FFN_ROWS = 512
FFN_COLS = 256
FFN_STEPS = D_FF // FFN_COLS
FFN_GROUP = 4


def _layernorm(x):
    mu = jnp.mean(x, axis=-1, keepdims=True)
    xc = x - mu
    var = jnp.mean(xc * xc, axis=-1, keepdims=True)
    return xc * lax.rsqrt(var + LN_EPS)


def _mod_kernel(c_ref, w_ref, b_ref, o_ref):
    c = c_ref[...]
    act = c * jax.nn.sigmoid(c)
    o_ref[...] = jnp.dot(act, w_ref[...], precision=lax.Precision.HIGHEST,
                         preferred_element_type=F32) + b_ref[...]


def _modulation(c, w_ada, b_ada):
    n = w_ada.shape[1]
    blk = D_MODEL
    return pl.pallas_call(
        _mod_kernel,
        out_shape=jax.ShapeDtypeStruct((BATCH, n), F32),
        grid=(n // blk,),
        in_specs=[pl.BlockSpec((BATCH, D_MODEL), lambda i: (0, 0)),
                  pl.BlockSpec((D_MODEL, blk), lambda i: (0, i)),
                  pl.BlockSpec((1, blk), lambda i: (0, i))],
        out_specs=pl.BlockSpec((BATCH, blk), lambda i: (0, i)),
        name="adaln_mod",
    )(c, w_ada, b_ada.reshape(1, n))


def _inproj_kernel(x_ref, sh_ref, sc_ref, wqt_ref, wk_ref, wvt_ref, ws_ref,
                   qt_ref, k_ref, vt_ref, s_ref):
    u = _layernorm(x_ref[0]) * (1.0 + sc_ref[0]) + sh_ref[0]
    ub = u.astype(BF16)
    nt = (((1,), (1,)), ((), ()))
    qt = lax.dot_general(wqt_ref[...], ub, nt, preferred_element_type=F32)
    qt_ref[0] = (qt * (HEAD_DIM ** -0.5 * LOG2E)).astype(BF16)
    k_ref[0] = jnp.dot(ub, wk_ref[...], preferred_element_type=F32).astype(BF16)
    vt = lax.dot_general(wvt_ref[...], ub, nt, preferred_element_type=F32).astype(BF16)
    ones = jnp.ones((V_ROWS - HEAD_COLS, vt.shape[1]), BF16)
    for h in range(ATTN_HEADS):
        vt_ref[0, h, 0:HEAD_COLS, :] = vt[h * HEAD_COLS:(h + 1) * HEAD_COLS, :]
        vt_ref[0, h, HEAD_COLS:V_ROWS, :] = ones
    s_ref[0] = jnp.dot(ub, ws_ref[...], preferred_element_type=F32)


def _in_projection(x, mod3, w_in_b):
    tm = PROJ_ROWS
    aw = ATTN_WIDTH
    wqt = jnp.transpose(w_in_b[:, 0:aw])
    wk = w_in_b[:, aw:2 * aw]
    wvt = jnp.transpose(w_in_b[:, 2 * aw:3 * aw])
    ws = w_in_b[:, 3 * aw:]
    const = lambda b, i: (0, 0)
    tposed = pl.BlockSpec((1, aw, tm), lambda b, i: (b, 0, i))
    rowblk = lambda w: pl.BlockSpec((1, tm, w), lambda b, i: (b, i, 0))
    return pl.pallas_call(
        _inproj_kernel,
        out_shape=(jax.ShapeDtypeStruct((BATCH, aw, SEQ), BF16),
                   jax.ShapeDtypeStruct((BATCH, SEQ, aw), BF16),
                   jax.ShapeDtypeStruct((BATCH, ATTN_HEADS, V_ROWS, SEQ), BF16),
                   jax.ShapeDtypeStruct((BATCH, SEQ, SSM_WIDTH), F32)),
        grid=(BATCH, SEQ // tm),
        in_specs=[rowblk(D_MODEL),
                  pl.BlockSpec((1, 1, D_MODEL), lambda b, i: (b, 0, 0)),
                  pl.BlockSpec((1, 1, D_MODEL), lambda b, i: (b, 0, 1)),
                  pl.BlockSpec((aw, D_MODEL), const), pl.BlockSpec((D_MODEL, aw), const),
                  pl.BlockSpec((aw, D_MODEL), const), pl.BlockSpec((D_MODEL, SSM_WIDTH), const)],
        out_specs=(tposed, rowblk(aw),
                   pl.BlockSpec((1, ATTN_HEADS, V_ROWS, tm), lambda b, i: (b, 0, 0, i)),
                   rowblk(SSM_WIDTH)),
        compiler_params=pltpu.CompilerParams(vmem_limit_bytes=VMEM_LIMIT_BYTES),
        name="ln_inproj",
    )(x, mod3, mod3, wqt, wk, wvt, ws)


def _t5_bucket_np(dist):
    max_exact = REL_BUCKETS // 2
    distf = np.maximum(dist, 1).astype(np.float32)
    large = max_exact + (np.log(distf / max_exact) / math.log(REL_MAX_DIST / max_exact)
                         * (REL_BUCKETS - max_exact)).astype(np.int32)
    large = np.minimum(large, REL_BUCKETS - 1)
    return np.where(dist < max_exact, dist, large).astype(np.int32)


def _bucket_tiles():
    kpos = np.arange(ATTN_TK)[:, None]
    qpos = np.arange(ATTN_TQ)[None, :]
    diag = qpos - kpos
    diag_b = np.where(diag >= 0, _t5_bucket_np(np.maximum(diag, 0)), -1)
    sub_b = _t5_bucket_np(diag + ATTN_TK)
    return np.stack([diag_b, sub_b]).astype(np.int32)


def _bias_kernel(tab_ref, bucket_ref, o_ref):
    h = pl.program_id(0)
    far = tab_ref[REL_BUCKETS - 1, h]
    for tile in range(2):
        b = bucket_ref[tile]
        acc = jnp.full(b.shape, NEG_INF, F32)
        for n in range(REL_BUCKETS):
            acc = jnp.where(b == n, (tab_ref[n, h] - far) * LOG2E, acc)
        o_ref[0, tile] = acc


def _bias_tiles(rel_bias):
    buckets = jnp.asarray(_bucket_tiles())
    return pl.pallas_call(
        _bias_kernel,
        out_shape=jax.ShapeDtypeStruct((ATTN_HEADS, 2, ATTN_TK, ATTN_TQ), F32),
        grid=(ATTN_HEADS,),
        in_specs=[pl.BlockSpec(memory_space=pltpu.SMEM),
                  pl.BlockSpec((2, ATTN_TK, ATTN_TQ), lambda h: (0, 0, 0))],
        out_specs=pl.BlockSpec((1, 2, ATTN_TK, ATTN_TQ), lambda h: (h, 0, 0, 0)),
        name="t5_bias_tiles",
    )(rel_bias, buckets)


def _attn_kernel(qt_ref, k_ref, vt_ref, bias_ref, lq1_ref, lk1_ref, lq2_ref, lk2_ref, g_ref,
                 o_ref, m_ref, acc_ref, qs_ref, *, lam_init):
    qi = pl.program_id(2)
    tq, tk = ATTN_TQ, ATTN_TK

    qt = qt_ref[0].astype(F32)
    row = lax.broadcasted_iota(jnp.int32, qt.shape, 0)
    qs_ref[0] = jnp.where(row < HEAD_DIM, qt, 0.0).astype(BF16)
    qs_ref[1] = jnp.where(row >= HEAD_DIM, qt, 0.0).astype(BF16)

    def scores(j):
        k = k_ref[0, pl.ds(pl.multiple_of(j * tk, tk), tk), :]
        return [jnp.dot(k, qs_ref[c], preferred_element_type=F32) for c in range(2)]

    def update(j, ss):
        vt = vt_ref[0, 0, :, pl.ds(pl.multiple_of(j * tk, tk), tk)]
        for c in range(2):
            s = ss[c]
            m_prev = m_ref[c]
            m_new = jnp.maximum(m_prev, jnp.max(s, axis=0, keepdims=True))
            alpha = jnp.exp2(m_prev - m_new)
            p = jnp.exp2(s - m_new).astype(BF16)
            acc_ref[c] = alpha * acc_ref[c] + jnp.dot(vt, p, preferred_element_type=F32)
            m_ref[c] = m_new

    m_ref[...] = jnp.full(m_ref.shape, NEG_INF, F32)
    acc_ref[...] = jnp.zeros_like(acc_ref)

    nfar = jnp.maximum(qi - 1, 0)

    def far_pair(i, carry):
        s_a = scores(2 * i)
        s_b = scores(2 * i + 1)
        update(2 * i, s_a)
        update(2 * i + 1, s_b)
        return carry

    lax.fori_loop(0, nfar // 2, far_pair, 0)

    def far_single(i, carry):
        update(nfar - 1, scores(nfar - 1))
        return carry

    lax.fori_loop(0, nfar % 2, far_single, 0)

    half = tk // 2

    def diag_scores():
        parts = []
        for part in range(2):
            koff = pl.multiple_of(qi * tk + part * half, half)
            k = k_ref[0, pl.ds(koff, half), :]
            q0 = part * half
            parts.append([jnp.dot(k, qs_ref[c, :, q0:tq], preferred_element_type=F32)
                          + bias_ref[0, 0, part * half:(part + 1) * half, q0:tq] for c in range(2)])
        return parts

    def diag_update(parts):
        for part in range(2):
            koff = pl.multiple_of(qi * tk + part * half, half)
            vt = vt_ref[0, 0, :, pl.ds(koff, half)]
            q0 = part * half
            for c in range(2):
                s = parts[part][c]
                m_prev = m_ref[c, :, q0:tq]
                m_new = jnp.maximum(m_prev, jnp.max(s, axis=0, keepdims=True))
                alpha = jnp.exp2(m_prev - m_new)
                p = jnp.exp2(s - m_new).astype(BF16)
                acc_ref[c, :, q0:tq] = (alpha * acc_ref[c, :, q0:tq]
                                        + jnp.dot(vt, p, preferred_element_type=F32))
                m_ref[c, :, q0:tq] = m_new

    def sub_and_diag(i, carry):
        s_sub = [s + bias_ref[0, 1] for s in scores(qi - 1)]
        parts = diag_scores()
        update(qi - 1, s_sub)
        diag_update(parts)
        return carry

    lax.fori_loop(0, jnp.minimum(qi, 1), sub_and_diag, 0)

    def diag_only(i, carry):
        diag_update(diag_scores())
        return carry

    lax.fori_loop(0, 1 - jnp.minimum(qi, 1), diag_only, 0)

    lam = (jnp.exp(jnp.sum(lq1_ref[...] * lk1_ref[...], axis=1, keepdims=True))
           - jnp.exp(jnp.sum(lq2_ref[...] * lk2_ref[...], axis=1, keepdims=True)) + lam_init)
    on = [acc_ref[c, 0:HEAD_COLS, :] / acc_ref[c, HEAD_COLS:HEAD_COLS + 1, :] for c in range(2)]
    ot = on[0] - lam * on[1]
    ms = jnp.mean(ot * ot, axis=0, keepdims=True)
    ot = ot * lax.rsqrt(ms + LN_EPS) * g_ref[...] * (1.0 - lam_init)
    o_ref[0] = jnp.transpose(ot).astype(BF16)


def _attention(qt, k, vt, bias_tiles, lam_q1, lam_k1, lam_q2, lam_k2, subln_g, lam_init):
    tq = ATTN_TQ
    small = lambda b, h, i: (0, 0)
    return pl.pallas_call(
        functools.partial(_attn_kernel, lam_init=lam_init),
        out_shape=jax.ShapeDtypeStruct((BATCH, SEQ, ATTN_WIDTH), BF16),
        grid=(BATCH, ATTN_HEADS, SEQ // tq),
        in_specs=[pl.BlockSpec((1, HEAD_COLS, tq), lambda b, h, i: (b, h, i)),
                  pl.BlockSpec((1, SEQ, HEAD_COLS), lambda b, h, i: (b, 0, h)),
                  pl.BlockSpec((1, 1, V_ROWS, SEQ), lambda b, h, i: (b, h, 0, 0)),
                  pl.BlockSpec((1, 2, ATTN_TK, tq), lambda b, h, i: (h, 0, 0, 0)),
                  pl.BlockSpec((1, HEAD_DIM), small), pl.BlockSpec((1, HEAD_DIM), small),
                  pl.BlockSpec((1, HEAD_DIM), small), pl.BlockSpec((1, HEAD_DIM), small),
                  pl.BlockSpec((HEAD_COLS, 1), small)],
        out_specs=pl.BlockSpec((1, tq, HEAD_COLS), lambda b, h, i: (b, i, h)),
        scratch_shapes=[pltpu.VMEM((2, 1, tq), F32),
                        pltpu.VMEM((2, V_ROWS, tq), F32),
                        pltpu.VMEM((2, HEAD_COLS, tq), BF16)],
        compiler_params=pltpu.CompilerParams(vmem_limit_bytes=VMEM_LIMIT_BYTES),
        name="diff_attention",
    )(qt, k, vt, bias_tiles, lam_q1, lam_k1, lam_q2, lam_k2, subln_g)


def _ssm_prep_kernel(are_ref, aim_ref, ldt_ref, ebr_ref, ebi_ref, ecr_ref, eci_ref,
                     eb_ref, ec_ref, abar_ref):
    ar = are_ref[0]
    ai = aim_ref[0]
    dt = jnp.exp(ldt_ref[0])
    mag = jnp.exp(ar * dt)
    ang = ai * dt
    pr = mag * jnp.cos(ang)
    pim = mag * jnp.sin(ang)
    den = ar * ar + ai * ai
    fr = ((pr - 1.0) * ar + pim * ai) / den
    fi = (pim * ar - (pr - 1.0) * ai) / den
    ebr = ebr_ref[0]
    ebi = ebi_ref[0]
    ns = SSM_HALF_STATE
    bre = fr * ebr - fi * ebi
    bim = fr * ebi + fi * ebr
    cre = ecr_ref[0].astype(BF16)
    cim = (-eci_ref[0]).astype(BF16)
    pows = [(jnp.ones_like(pr), jnp.zeros_like(pr)), (pr, pim)]
    for _ in range(2, SSM_BLOCK + 1):
        xr, xi = pows[-1]
        pows.append((xr * pr - xi * pim, xr * pim + xi * pr))
    for i in range(SSM_BLOCK):
        xr, xi = pows[SSM_BLOCK - 1 - i]
        blk_r = (bre * xr - bim * xi).astype(BF16)
        blk_i = (bre * xi + bim * xr).astype(BF16)
        rows = slice(i * SSM_HALF, (i + 1) * SSM_HALF)
        for k in range(ns // LANES):
            src = slice(k * LANES, (k + 1) * LANES)
            eb_ref[0, rows, 2 * k * LANES:(2 * k + 1) * LANES] = blk_r[:, src]
            eb_ref[0, rows, (2 * k + 1) * LANES:(2 * k + 2) * LANES] = blk_i[:, src]
    for k in range(ns // LANES):
        src = slice(k * LANES, (k + 1) * LANES)
        ec_ref[0, 2 * k * LANES:(2 * k + 1) * LANES, :] = cre[src, :]
        ec_ref[0, (2 * k + 1) * LANES:(2 * k + 2) * LANES, :] = cim[src, :]
    qr, qi = pr, pim
    for _ in range(int(math.log2(SSM_CHUNK))):
        qr, qi = qr * qr - qi * qi, 2.0 * qr * qi
    br, bi = pows[SSM_BLOCK]
    abar_ref[0] = jnp.concatenate([pr, pim, qr, qi, br, bi, jnp.zeros((2, ns), F32)], axis=0)


def _block_diag_half(w):
    g2 = SSM_GROUPS // 2
    w = w.reshape(2, g2, w.shape[1], w.shape[2])
    eye = jnp.eye(g2, dtype=w.dtype)
    out = w[:, :, :, None, :] * eye[None, :, None, :, None]
    return out.reshape(2, g2 * w.shape[2], g2 * w.shape[3])


def _ssm_prep(a_re, a_im, b_re, b_im, c_re, c_im, log_dt):
    ns = SSM_HALF_STATE
    row = lambda v: v.reshape(2, 1, ns)
    ldt = jnp.broadcast_to(log_dt[:, None], (SSM_GROUPS, SSM_STATE))
    ebr = _block_diag_half(jnp.transpose(b_re, (0, 2, 1)))
    ebi = _block_diag_half(jnp.transpose(b_im, (0, 2, 1)))
    ecr = _block_diag_half(jnp.transpose(c_re, (0, 2, 1)))
    eci = _block_diag_half(jnp.transpose(c_im, (0, 2, 1)))
    vec = pl.BlockSpec((1, 1, ns), lambda j: (j, 0, 0))
    return pl.pallas_call(
        _ssm_prep_kernel,
        out_shape=(jax.ShapeDtypeStruct((2, SSM_BLOCK * SSM_HALF, 2 * ns), BF16),
                   jax.ShapeDtypeStruct((2, 2 * ns, SSM_HALF), BF16),
                   jax.ShapeDtypeStruct((2, 8, ns), F32)),
        grid=(2,),
        in_specs=[vec, vec, vec,
                  pl.BlockSpec((1, SSM_HALF, ns), lambda j: (j, 0, 0)),
                  pl.BlockSpec((1, SSM_HALF, ns), lambda j: (j, 0, 0)),
                  pl.BlockSpec((1, ns, SSM_HALF), lambda j: (j, 0, 0)),
                  pl.BlockSpec((1, ns, SSM_HALF), lambda j: (j, 0, 0))],
        out_specs=(pl.BlockSpec((1, SSM_BLOCK * SSM_HALF, 2 * ns), lambda j: (j, 0, 0)),
                   pl.BlockSpec((1, 2 * ns, SSM_HALF), lambda j: (j, 0, 0)),
                   pl.BlockSpec((1, 8, ns), lambda j: (j, 0, 0))),
        name="ssm_discretise",
    )(row(a_re), row(a_im), row(ldt), ebr, ebi, ecr, eci)


def _ssm_kernel(s0_ref, s1_ref, s2_ref, s3_ref, eb_ref, ec_ref, abar_ref, d_ref, wglu_ref, bglu_ref,
                y0_ref, y1_ref, y2_ref, y3_ref, h_ref, carry_ref):
    nb, nr, nt, ns = BATCH, SSM_ROWS, SSM_CHUNK, SSM_HALF_STATE
    rows = nb * nr
    s_refs = (s0_ref, s1_ref, s2_ref, s3_ref)
    y_refs = (y0_ref, y1_ref, y2_ref, y3_ref)

    @pl.when(pl.program_id(0) == 0)
    def _():
        carry_ref[...] = jnp.zeros_like(carry_ref)

    def s_tile(t, lo, width):
        step = pl.ds(t, nr, stride=nt)
        slabs = [s_refs[k][:, step, :].reshape(rows, LANES)
                 for k in range(lo // LANES, (lo + width) // LANES)]
        return jnp.concatenate(slabs, axis=1)

    ntile = ns // LANES

    strip = 2 * LANES

    def advance(j, s_j, nsteps):
        mul_row = {1: 0, SSM_BLOCK: 4}[nsteps]
        w_rows = slice((SSM_BLOCK - nsteps) * SSM_HALF, SSM_BLOCK * SSM_HALF)
        tiles = [None] * (2 * ntile)
        for k in range(ntile):
            lo = k * LANES
            bu = jnp.dot(s_j, eb_ref[j, w_rows, k * strip:(k + 1) * strip], preferred_element_type=F32)
            ar = abar_ref[j, mul_row:mul_row + 1, lo:lo + LANES]
            ai = abar_ref[j, mul_row + 1:mul_row + 2, lo:lo + LANES]
            hr = h_ref[j, 2 * k]
            hi = h_ref[j, 2 * k + 1]
            tiles[2 * k] = ar * hr - ai * hi + bu[:, 0:LANES]
            tiles[2 * k + 1] = ar * hi + ai * hr + bu[:, LANES:strip]
            h_ref[j, 2 * k] = tiles[2 * k]
            h_ref[j, 2 * k + 1] = tiles[2 * k + 1]
        return tiles

    h_ref[...] = jnp.zeros_like(h_ref)

    def local_block(q, carry):
        for j in range(2):
            s_j = jnp.concatenate([s_tile(q * SSM_BLOCK + i, j * SSM_HALF, SSM_HALF)
                                   for i in range(SSM_BLOCK)], axis=1).astype(BF16)
            advance(j, s_j, SSM_BLOCK)
        return carry

    lax.fori_loop(0, nt // SSM_BLOCK, local_block, 0)

    for j in range(2):
        for k in range(ntile):
            lo = k * LANES
            qr = abar_ref[j, 2:3, lo:lo + LANES]
            qi = abar_ref[j, 3:4, lo:lo + LANES]
            st_r = carry_ref[j, 2 * k]
            st_i = carry_ref[j, 2 * k + 1]
            for c in range(nr):
                sel = pl.ds(c, nb, stride=nr)
                loc_r = h_ref[j, 2 * k, sel, :]
                loc_i = h_ref[j, 2 * k + 1, sel, :]
                h_ref[j, 2 * k, sel, :] = st_r
                h_ref[j, 2 * k + 1, sel, :] = st_i
                st_r, st_i = qr * st_r - qi * st_i + loc_r, qr * st_i + qi * st_r + loc_i
            carry_ref[j, 2 * k] = st_r
            carry_ref[j, 2 * k + 1] = st_i

    def output_step(t, carry):
        s_t = s_tile(t, 0, SSM_WIDTH)
        ys = []
        for j in range(2):
            tiles = advance(j, s_t[:, j * SSM_HALF:(j + 1) * SSM_HALF].astype(BF16), 1)
            ys.append(jnp.dot(jnp.concatenate(tiles, axis=1).astype(BF16), ec_ref[j],
                              preferred_element_type=F32))
        y = jnp.concatenate(ys, axis=1) + d_ref[...] * s_t
        y = jax.nn.gelu(y)
        z = jnp.dot(y.astype(BF16), wglu_ref[...], preferred_element_type=F32) + bglu_ref[...]
        out = y * jax.nn.sigmoid(z)
        step = pl.ds(t, nr, stride=nt)
        for k in range(SSM_WIDTH // LANES):
            y_refs[k][:, step, :] = out[:, k * LANES:(k + 1) * LANES].reshape(nb, nr, LANES)
        return carry

    lax.fori_loop(0, nt, output_step, 0)


def _ssm(s, eb, ec, abar, d_skip, w_glu_b, b_glu):
    nt, nr, ns = SSM_CHUNK, SSM_ROWS, SSM_HALF_STATE
    nslab = SSM_WIDTH // LANES
    const3 = lambda i: (0, 0, 0)
    const2 = lambda i: (0, 0)
    once = dict(pipeline_mode=pl.Buffered(1))
    slab = lambda k: pl.BlockSpec((BATCH, nr * nt, LANES), lambda i: (0, i, k))
    return pl.pallas_call(
        _ssm_kernel,
        out_shape=tuple(jax.ShapeDtypeStruct((BATCH, SEQ, LANES), F32) for _ in range(nslab)),
        grid=(SEQ // (nr * nt),),
        in_specs=[slab(k) for k in range(nslab)] + [
            pl.BlockSpec((2, SSM_BLOCK * SSM_HALF, 2 * ns), const3, **once),
            pl.BlockSpec((2, 2 * ns, SSM_HALF), const3, **once),
            pl.BlockSpec((2, 8, ns), const3),
            pl.BlockSpec((1, SSM_WIDTH), const2),
            pl.BlockSpec((SSM_WIDTH, SSM_WIDTH), const2),
            pl.BlockSpec((1, SSM_WIDTH), const2)],
        out_specs=tuple(slab(0) for _ in range(nslab)),
        scratch_shapes=[pltpu.VMEM((2, 2 * ns // LANES, BATCH * nr, LANES), F32),
                        pltpu.VMEM((2, 2 * ns // LANES, BATCH, LANES), F32)],
        compiler_params=pltpu.CompilerParams(dimension_semantics=("arbitrary",),
                                             vmem_limit_bytes=VMEM_LIMIT_BYTES),
        name="s5_ssm_glu",
    )(s, s, s, s, eb, ec, abar, d_skip.reshape(1, SSM_WIDTH), w_glu_b, b_glu.reshape(1, SSM_WIDTH))


def _ffn_kernel(x_ref, o_ref, y0_ref, y1_ref, y2_ref, y3_ref, g1_ref, sh2_ref, sc2_ref, g2_ref, wo_ref,
                ln1g_ref, ln1b_ref, wup_ref, cw_ref, wdn_ref, ln2g_ref, ln2b_ref, out_ref, carry_ref):
    tm = FFN_ROWS

    @pl.when(pl.program_id(1) == 0)
    def _():
        carry_ref[...] = jnp.zeros_like(carry_ref)

    y = jnp.concatenate([y0_ref[0], y1_ref[0], y2_ref[0], y3_ref[0]], axis=1).astype(BF16)
    m = (jnp.dot(o_ref[0], wo_ref[0:ATTN_WIDTH, :], preferred_element_type=F32)
         + jnp.dot(y, wo_ref[ATTN_WIDTH:, :], preferred_element_type=F32))
    x1 = _layernorm(DEEPNORM_ALPHA * x_ref[0] + g1_ref[0] * m) * ln1g_ref[...] + ln1b_ref[...]
    ub = (_layernorm(x1) * (1.0 + sc2_ref[0]) + sh2_ref[0]).astype(BF16)

    halo = [carry_ref[idx] for idx in range(2 * FFN_STEPS)]
    tails = [None] * (2 * FFN_STEPS)

    def up(idx):
        return jnp.dot(ub, wup_ref[idx], preferred_element_type=F32)

    def conv_cols(idx, hcur):
        ext = jnp.concatenate([halo[idx], hcur], axis=0)
        tails[idx] = hcur[tm - SUBLANES:tm, :]
        w = cw_ref[idx]
        return (w[2:3] * hcur + w[1:2] * ext[SUBLANES - 1:SUBLANES - 1 + tm, :]
                + w[0:1] * ext[SUBLANES - 2:SUBLANES - 2 + tm, :] + w[3:4])

    f = None
    group = []
    nxt = (up(0), up(FFN_STEPS))
    for j in range(FFN_STEPS):
        cur = nxt
        if j + 1 < FFN_STEPS:
            nxt = (up(j + 1), up(j + 1 + FFN_STEPS))
        val = conv_cols(j, cur[0])
        gate = conv_cols(j + FFN_STEPS, cur[1])
        group.append((gate * jax.nn.sigmoid(gate) * val).astype(BF16))
        if len(group) < FFN_GROUP and j + 1 < FFN_STEPS:
            continue
        lo = (j + 1 - len(group)) * FFN_COLS
        a = jnp.concatenate(group, axis=1) if len(group) > 1 else group[0]
        part = jnp.dot(a, wdn_ref[lo:(j + 1) * FFN_COLS, :], preferred_element_type=F32)
        f = part if f is None else f + part
        group = []
    for idx in range(2 * FFN_STEPS):
        carry_ref[idx] = tails[idx]
    out_ref[0] = (_layernorm(DEEPNORM_ALPHA * x1 + g2_ref[0] * f) * ln2g_ref[...] + ln2b_ref[...])


def _outproj_ffn(x, o, y_slabs, mod3, w_out_b, ln1_g, ln1_b, w_up_c, conv_c, w_down_c, ln2_g, ln2_b):
    tm = FFN_ROWS
    row = lambda b, i: (b, i, 0)
    modv = lambda k: pl.BlockSpec((1, 1, D_MODEL), lambda b, i: (b, 0, k))
    c2 = lambda b, i: (0, 0)
    c3 = lambda b, i: (0, 0, 0)
    vec = pl.BlockSpec((1, D_MODEL), c2)
    return pl.pallas_call(
        _ffn_kernel,
        out_shape=jax.ShapeDtypeStruct((BATCH, SEQ, D_MODEL), F32),
        grid=(BATCH, SEQ // tm),
        in_specs=[pl.BlockSpec((1, tm, D_MODEL), row),
                  pl.BlockSpec((1, tm, ATTN_WIDTH), row),
                  *[pl.BlockSpec((1, tm, LANES), row) for _ in range(SSM_WIDTH // LANES)],
                  modv(2), modv(3), modv(4), modv(5),
                  pl.BlockSpec((D_MODEL, D_MODEL), c2, pipeline_mode=pl.Buffered(1)),
                  vec, vec,
                  pl.BlockSpec((2 * FFN_STEPS, D_MODEL, FFN_COLS), c3, pipeline_mode=pl.Buffered(1)),
                  pl.BlockSpec((2 * FFN_STEPS, SUBLANES, FFN_COLS), c3),
                  pl.BlockSpec((D_FF, D_MODEL), c2, pipeline_mode=pl.Buffered(1)),
                  vec, vec],
        out_specs=pl.BlockSpec((1, tm, D_MODEL), row),
        scratch_shapes=[pltpu.VMEM((2 * FFN_STEPS, SUBLANES, FFN_COLS), F32)],
        compiler_params=pltpu.CompilerParams(dimension_semantics=("arbitrary", "arbitrary"),
                                             vmem_limit_bytes=VMEM_LIMIT_BYTES),
        name="outproj_ffn",
    )(x, o, *y_slabs, mod3, mod3, mod3, mod3, w_out_b, ln1_g, ln1_b, w_up_c, conv_c, w_down_c, ln2_g, ln2_b)


def kernel(x, c, w_ada, b_ada, w_in, lam_q1, lam_k1, lam_q2, lam_k2, subln_g, ssm_a_re, ssm_a_im,
           ssm_b_re, ssm_b_im, ssm_c_re, ssm_c_im, ssm_d, ssm_log_dt, w_glu, b_glu, w_out, ln1_g,
           ln1_b, w_up, conv_w, conv_b, w_down, ln2_g, ln2_b, rel_bias):
    bias_tiles = _bias_tiles(rel_bias)
    for layer in range(DEPTH):
        lam_init = 0.8 - 0.6 * math.exp(-0.3 * layer)
        mod3 = _modulation(c, w_ada[layer], b_ada[layer]).reshape(BATCH, 1, 6 * D_MODEL)
        qt, k, vt, s = _in_projection(x, mod3, w_in[layer].astype(BF16))
        o = _attention(qt, k, vt, bias_tiles, lam_q1[layer][None], lam_k1[layer][None],
                       lam_q2[layer][None], lam_k2[layer][None], subln_g[layer][:, None], lam_init)
        eb, ec, abar = _ssm_prep(ssm_a_re[layer], ssm_a_im[layer], ssm_b_re[layer], ssm_b_im[layer],
                                 ssm_c_re[layer], ssm_c_im[layer], ssm_log_dt[layer])
        y = _ssm(s, eb, ec, abar, ssm_d[layer], w_glu[layer].astype(BF16), b_glu[layer])
        w_up_c = jnp.transpose(w_up[layer].astype(BF16).reshape(D_MODEL, 2 * FFN_STEPS, FFN_COLS),
                               (1, 0, 2))
        conv_c = jnp.concatenate([conv_w[layer], conv_b[layer][None],
                                  jnp.zeros((SUBLANES - 4, 2 * D_FF), F32)], axis=0)
        conv_c = jnp.transpose(conv_c.reshape(SUBLANES, 2 * FFN_STEPS, FFN_COLS), (1, 0, 2))
        x = _outproj_ffn(x, o, y, mod3, w_out[layer].astype(BF16), ln1_g[layer][None],
                         ln1_b[layer][None], w_up_c, conv_c, w_down[layer].astype(BF16),
                         ln2_g[layer][None],
                         ln2_b[layer][None])
    return x
```

```python
import functools
import math

import numpy as np
import jax
import jax.numpy as jnp
from jax import lax
from jax.experimental import pallas as pl
from jax.experimental.pallas import tpu as pltpu

F32 = jnp.float32
BF16 = jnp.bfloat16

D_MODEL = 1024
BATCH = 16
SEQ = 4096
ATTN_WIDTH = 512
SSM_WIDTH = 512
ATTN_HEADS = 4
HEAD_DIM = 64
HEAD_COLS = 2 * HEAD_DIM
SSM_GROUP = 16
SSM_GROUPS = 32
SSM_STATE = 64
D_FF = 2816
REL_BUCKETS = 32
REL_MAX_DIST = 128
LN_EPS = 1e-5
NEG_INF = -1e30
DEPTH = 1
DEEPNORM_ALPHA = (2 * DEPTH) ** 0.25

LANES = 128
SUBLANES = 8
MXU_DIM = 256
VMEM_LIMIT_BYTES = 56 * 1024 * 1024

PROJ_ROWS = 512
ATTN_TQ = 512
ATTN_TK = 512
V_ROWS = HEAD_COLS + 16
LOG2E = math.log2(math.e)
SSM_CHUNK = 16
SSM_ROWS = 16
SSM_BLOCK = 4
SSM_HALF = SSM_WIDTH // 2
SSM_HALF_STATE = (SSM_GROUPS // 2) * SSM_STATE
FFN_ROWS = 512
FFN_COLS = 256
FFN_STEPS = D_FF // FFN_COLS
FFN_GROUP = FFN_STEPS


def _layernorm(x):
    mu = jnp.mean(x, axis=-1, keepdims=True)
    xc = x - mu
    var = jnp.mean(xc * xc, axis=-1, keepdims=True)
    return xc * lax.rsqrt(var + LN_EPS)


def _mod_kernel(c_ref, w_ref, b_ref, o_ref):
    c = c_ref[...]
    act = c * jax.nn.sigmoid(c)
    o_ref[...] = jnp.dot(act, w_ref[...], precision=lax.Precision.HIGHEST,
                         preferred_element_type=F32) + b_ref[...]


def _modulation(c, w_ada, b_ada):
    n = w_ada.shape[1]
    blk = D_MODEL
    return pl.pallas_call(
        _mod_kernel,
        out_shape=jax.ShapeDtypeStruct((BATCH, n), F32),
        grid=(n // blk,),
        in_specs=[pl.BlockSpec((BATCH, D_MODEL), lambda i: (0, 0)),
                  pl.BlockSpec((D_MODEL, blk), lambda i: (0, i)),
                  pl.BlockSpec((1, blk), lambda i: (0, i))],
        out_specs=pl.BlockSpec((BATCH, blk), lambda i: (0, i)),
        name="adaln_mod",
    )(c, w_ada, b_ada.reshape(1, n))


def _inproj_kernel(x_ref, sh_ref, sc_ref, wqt_ref, wk_ref, wvt_ref, ws_ref,
                   qt_ref, k_ref, vt_ref, s_ref):
    u = _layernorm(x_ref[0]) * (1.0 + sc_ref[0]) + sh_ref[0]
    ub = u.astype(BF16)
    nt = (((1,), (1,)), ((), ()))
    qt = lax.dot_general(wqt_ref[...], ub, nt, preferred_element_type=F32)
    qt_ref[0] = (qt * (HEAD_DIM ** -0.5 * LOG2E)).astype(BF16)
    k_ref[0] = jnp.dot(ub, wk_ref[...], preferred_element_type=F32).astype(BF16)
    vt = lax.dot_general(wvt_ref[...], ub, nt, preferred_element_type=F32).astype(BF16)
    ones = jnp.ones((V_ROWS - HEAD_COLS, vt.shape[1]), BF16)
    for h in range(ATTN_HEADS):
        vt_ref[0, h, 0:HEAD_COLS, :] = vt[h * HEAD_COLS:(h + 1) * HEAD_COLS, :]
        vt_ref[0, h, HEAD_COLS:V_ROWS, :] = ones
    s_ref[0] = jnp.dot(ub, ws_ref[...], preferred_element_type=F32)


def _in_projection(x, mod3, w_in_b):
    tm = PROJ_ROWS
    aw = ATTN_WIDTH
    wqt = jnp.transpose(w_in_b[:, 0:aw])
    wk = w_in_b[:, aw:2 * aw]
    wvt = jnp.transpose(w_in_b[:, 2 * aw:3 * aw])
    ws = w_in_b[:, 3 * aw:]
    const = lambda b, i: (0, 0)
    tposed = pl.BlockSpec((1, aw, tm), lambda b, i: (b, 0, i))
    rowblk = lambda w: pl.BlockSpec((1, tm, w), lambda b, i: (b, i, 0))
    return pl.pallas_call(
        _inproj_kernel,
        out_shape=(jax.ShapeDtypeStruct((BATCH, aw, SEQ), BF16),
                   jax.ShapeDtypeStruct((BATCH, SEQ, aw), BF16),
                   jax.ShapeDtypeStruct((BATCH, ATTN_HEADS, V_ROWS, SEQ), BF16),
                   jax.ShapeDtypeStruct((BATCH, SEQ, SSM_WIDTH), F32)),
        grid=(BATCH, SEQ // tm),
        in_specs=[rowblk(D_MODEL),
                  pl.BlockSpec((1, 1, D_MODEL), lambda b, i: (b, 0, 0)),
                  pl.BlockSpec((1, 1, D_MODEL), lambda b, i: (b, 0, 1)),
                  pl.BlockSpec((aw, D_MODEL), const), pl.BlockSpec((D_MODEL, aw), const),
                  pl.BlockSpec((aw, D_MODEL), const), pl.BlockSpec((D_MODEL, SSM_WIDTH), const)],
        out_specs=(tposed, rowblk(aw),
                   pl.BlockSpec((1, ATTN_HEADS, V_ROWS, tm), lambda b, i: (b, 0, 0, i)),
                   rowblk(SSM_WIDTH)),
        compiler_params=pltpu.CompilerParams(vmem_limit_bytes=VMEM_LIMIT_BYTES),
        name="ln_inproj",
    )(x, mod3, mod3, wqt, wk, wvt, ws)


def _t5_bucket_np(dist):
    max_exact = REL_BUCKETS // 2
    distf = np.maximum(dist, 1).astype(np.float32)
    large = max_exact + (np.log(distf / max_exact) / math.log(REL_MAX_DIST / max_exact)
                         * (REL_BUCKETS - max_exact)).astype(np.int32)
    large = np.minimum(large, REL_BUCKETS - 1)
    return np.where(dist < max_exact, dist, large).astype(np.int32)


def _bucket_tiles():
    kpos = np.arange(ATTN_TK)[:, None]
    qpos = np.arange(ATTN_TQ)[None, :]
    diag = qpos - kpos
    diag_b = np.where(diag >= 0, _t5_bucket_np(np.maximum(diag, 0)), -1)
    sub_b = _t5_bucket_np(diag + ATTN_TK)
    return np.stack([diag_b, sub_b]).astype(np.int32)


def _bias_kernel(tab_ref, bucket_ref, o_ref):
    h = pl.program_id(0)
    far = tab_ref[REL_BUCKETS - 1, h]
    for tile in range(2):
        b = bucket_ref[tile]
        acc = jnp.full(b.shape, NEG_INF, F32)
        for n in range(REL_BUCKETS):
            acc = jnp.where(b == n, (tab_ref[n, h] - far) * LOG2E, acc)
        o_ref[0, tile] = acc


def _bias_tiles(rel_bias):
    buckets = jnp.asarray(_bucket_tiles())
    return pl.pallas_call(
        _bias_kernel,
        out_shape=jax.ShapeDtypeStruct((ATTN_HEADS, 2, ATTN_TK, ATTN_TQ), F32),
        grid=(ATTN_HEADS,),
        in_specs=[pl.BlockSpec(memory_space=pltpu.SMEM),
                  pl.BlockSpec((2, ATTN_TK, ATTN_TQ), lambda h: (0, 0, 0))],
        out_specs=pl.BlockSpec((1, 2, ATTN_TK, ATTN_TQ), lambda h: (h, 0, 0, 0)),
        name="t5_bias_tiles",
    )(rel_bias, buckets)


def _attn_kernel(qt_ref, k_ref, vt_ref, bias_ref, lq1_ref, lk1_ref, lq2_ref, lk2_ref, g_ref,
                 o_ref, m_ref, acc_ref, qs_ref, *, lam_init):
    qi = pl.program_id(2)
    tq, tk = ATTN_TQ, ATTN_TK

    qt = qt_ref[0].astype(F32)
    row = lax.broadcasted_iota(jnp.int32, qt.shape, 0)
    qs_ref[0] = jnp.where(row < HEAD_DIM, qt, 0.0).astype(BF16)
    qs_ref[1] = jnp.where(row >= HEAD_DIM, qt, 0.0).astype(BF16)

    def scores(j):
        k = k_ref[0, pl.ds(pl.multiple_of(j * tk, tk), tk), :]
        return [jnp.dot(k, qs_ref[c], preferred_element_type=F32) for c in range(2)]

    def update(j, ss):
        vt = vt_ref[0, 0, :, pl.ds(pl.multiple_of(j * tk, tk), tk)]
        for c in range(2):
            s = ss[c]
            m_prev = m_ref[c]
            m_new = jnp.maximum(m_prev, jnp.max(s, axis=0, keepdims=True))
            alpha = jnp.exp2(m_prev - m_new)
            p = jnp.exp2(s - m_new).astype(BF16)
            acc_ref[c] = alpha * acc_ref[c] + jnp.dot(vt, p, preferred_element_type=F32)
            m_ref[c] = m_new

    m_ref[...] = jnp.full(m_ref.shape, NEG_INF, F32)
    acc_ref[...] = jnp.zeros_like(acc_ref)

    nfar = jnp.maximum(qi - 1, 0)

    def far_pair(i, carry):
        s_a = scores(2 * i)
        s_b = scores(2 * i + 1)
        update(2 * i, s_a)
        update(2 * i + 1, s_b)
        return carry

    lax.fori_loop(0, nfar // 2, far_pair, 0)

    half = tk // 2

    def diag_scores():
        parts = []
        for part in range(2):
            koff = pl.multiple_of(qi * tk + part * half, half)
            k = k_ref[0, pl.ds(koff, half), :]
            q0 = part * half
            parts.append([jnp.dot(k, qs_ref[c, :, q0:tq], preferred_element_type=F32)
                          + bias_ref[0, 0, part * half:(part + 1) * half, q0:tq] for c in range(2)])
        return parts

    def diag_update(parts):
        for part in range(2):
            koff = pl.multiple_of(qi * tk + part * half, half)
            vt = vt_ref[0, 0, :, pl.ds(koff, half)]
            q0 = part * half
            for c in range(2):
                s = parts[part][c]
                m_prev = m_ref[c, :, q0:tq]
                m_new = jnp.maximum(m_prev, jnp.max(s, axis=0, keepdims=True))
                alpha = jnp.exp2(m_prev - m_new)
                p = jnp.exp2(s - m_new).astype(BF16)
                acc_ref[c, :, q0:tq] = (alpha * acc_ref[c, :, q0:tq]
                                        + jnp.dot(vt, p, preferred_element_type=F32))
                m_ref[c, :, q0:tq] = m_new

    def sub_and_diag(i, carry):
        s_sub = [s + bias_ref[0, 1] for s in scores(qi - 1)]
        parts = diag_scores()
        update(qi - 1, s_sub)
        diag_update(parts)
        return carry

    def far_sub_and_diag(i, carry):
        s_far = scores(nfar - 1)
        s_sub = [s + bias_ref[0, 1] for s in scores(qi - 1)]
        parts = diag_scores()
        update(nfar - 1, s_far)
        update(qi - 1, s_sub)
        diag_update(parts)
        return carry

    odd = nfar % 2
    lax.fori_loop(0, odd, far_sub_and_diag, 0)
    lax.fori_loop(0, jnp.minimum(qi, 1) * (1 - odd), sub_and_diag, 0)

    def diag_only(i, carry):
        diag_update(diag_scores())
        return carry

    lax.fori_loop(0, 1 - jnp.minimum(qi, 1), diag_only, 0)

    lam = (jnp.exp(jnp.sum(lq1_ref[...] * lk1_ref[...], axis=1, keepdims=True))
           - jnp.exp(jnp.sum(lq2_ref[...] * lk2_ref[...], axis=1, keepdims=True)) + lam_init)
    on = [acc_ref[c, 0:HEAD_COLS, :] / acc_ref[c, HEAD_COLS:HEAD_COLS + 1, :] for c in range(2)]
    ot = on[0] - lam * on[1]
    ms = jnp.mean(ot * ot, axis=0, keepdims=True)
    ot = ot * lax.rsqrt(ms + LN_EPS) * g_ref[...] * (1.0 - lam_init)
    o_ref[0] = jnp.transpose(ot).astype(BF16)


def _attention(qt, k, vt, bias_tiles, lam_q1, lam_k1, lam_q2, lam_k2, subln_g, lam_init):
    tq = ATTN_TQ
    small = lambda b, h, i: (0, 0)
    return pl.pallas_call(
        functools.partial(_attn_kernel, lam_init=lam_init),
        out_shape=jax.ShapeDtypeStruct((BATCH, SEQ, ATTN_WIDTH), BF16),
        grid=(BATCH, ATTN_HEADS, SEQ // tq),
        in_specs=[pl.BlockSpec((1, HEAD_COLS, tq), lambda b, h, i: (b, h, i)),
                  pl.BlockSpec((1, SEQ, HEAD_COLS), lambda b, h, i: (b, 0, h)),
                  pl.BlockSpec((1, 1, V_ROWS, SEQ), lambda b, h, i: (b, h, 0, 0)),
                  pl.BlockSpec((1, 2, ATTN_TK, tq), lambda b, h, i: (h, 0, 0, 0)),
                  pl.BlockSpec((1, HEAD_DIM), small), pl.BlockSpec((1, HEAD_DIM), small),
                  pl.BlockSpec((1, HEAD_DIM), small), pl.BlockSpec((1, HEAD_DIM), small),
                  pl.BlockSpec((HEAD_COLS, 1), small)],
        out_specs=pl.BlockSpec((1, tq, HEAD_COLS), lambda b, h, i: (b, i, h)),
        scratch_shapes=[pltpu.VMEM((2, 1, tq), F32),
                        pltpu.VMEM((2, V_ROWS, tq), F32),
                        pltpu.VMEM((2, HEAD_COLS, tq), BF16)],
        compiler_params=pltpu.CompilerParams(vmem_limit_bytes=VMEM_LIMIT_BYTES),
        name="diff_attention",
    )(qt, k, vt, bias_tiles, lam_q1, lam_k1, lam_q2, lam_k2, subln_g)


def _ssm_prep_kernel(are_ref, aim_ref, ldt_ref, ebr_ref, ebi_ref, ecr_ref, eci_ref,
                     eb_ref, ec_ref, abar_ref):
    ar = are_ref[0]
    ai = aim_ref[0]
    dt = jnp.exp(ldt_ref[0])
    mag = jnp.exp(ar * dt)
    ang = ai * dt
    pr = mag * jnp.cos(ang)
    pim = mag * jnp.sin(ang)
    den = ar * ar + ai * ai
    fr = ((pr - 1.0) * ar + pim * ai) / den
    fi = (pim * ar - (pr - 1.0) * ai) / den
    ebr = ebr_ref[0]
    ebi = ebi_ref[0]
    ns = SSM_HALF_STATE
    bre = fr * ebr - fi * ebi
    bim = fr * ebi + fi * ebr
    cre = ecr_ref[0].astype(BF16)
    cim = (-eci_ref[0]).astype(BF16)
    pows = [(jnp.ones_like(pr), jnp.zeros_like(pr)), (pr, pim)]
    for _ in range(2, SSM_BLOCK + 1):
        xr, xi = pows[-1]
        pows.append((xr * pr - xi * pim, xr * pim + xi * pr))
    for i in range(SSM_BLOCK):
        xr, xi = pows[SSM_BLOCK - 1 - i]
        blk_r = (bre * xr - bim * xi).astype(BF16)
        blk_i = (bre * xi + bim * xr).astype(BF16)
        rows = slice(i * SSM_HALF, (i + 1) * SSM_HALF)
        for k in range(ns // LANES):
            src = slice(k * LANES, (k + 1) * LANES)
            eb_ref[0, rows, 2 * k * LANES:(2 * k + 1) * LANES] = blk_r[:, src]
            eb_ref[0, rows, (2 * k + 1) * LANES:(2 * k + 2) * LANES] = blk_i[:, src]
    for k in range(ns // LANES):
        src = slice(k * LANES, (k + 1) * LANES)
        ec_ref[0, 2 * k * LANES:(2 * k + 1) * LANES, :] = cre[src, :]
        ec_ref[0, (2 * k + 1) * LANES:(2 * k + 2) * LANES, :] = cim[src, :]
    qr, qi = pr, pim
    for _ in range(int(math.log2(SSM_CHUNK))):
        qr, qi = qr * qr - qi * qi, 2.0 * qr * qi
    br, bi = pows[SSM_BLOCK]
    abar_ref[0] = jnp.concatenate([pr, pim, qr, qi, br, bi, jnp.zeros((2, ns), F32)], axis=0)


def _block_diag_half(w):
    g2 = SSM_GROUPS // 2
    w = w.reshape(2, g2, w.shape[1], w.shape[2])
    eye = jnp.eye(g2, dtype=w.dtype)
    out = w[:, :, :, None, :] * eye[None, :, None, :, None]
    return out.reshape(2, g2 * w.shape[2], g2 * w.shape[3])


def _ssm_prep(a_re, a_im, b_re, b_im, c_re, c_im, log_dt):
    ns = SSM_HALF_STATE
    row = lambda v: v.reshape(2, 1, ns)
    ldt = jnp.broadcast_to(log_dt[:, None], (SSM_GROUPS, SSM_STATE))
    ebr = _block_diag_half(jnp.transpose(b_re, (0, 2, 1)))
    ebi = _block_diag_half(jnp.transpose(b_im, (0, 2, 1)))
    ecr = _block_diag_half(jnp.transpose(c_re, (0, 2, 1)))
    eci = _block_diag_half(jnp.transpose(c_im, (0, 2, 1)))
    vec = pl.BlockSpec((1, 1, ns), lambda j: (j, 0, 0))
    return pl.pallas_call(
        _ssm_prep_kernel,
        out_shape=(jax.ShapeDtypeStruct((2, SSM_BLOCK * SSM_HALF, 2 * ns), BF16),
                   jax.ShapeDtypeStruct((2, 2 * ns, SSM_HALF), BF16),
                   jax.ShapeDtypeStruct((2, 8, ns), F32)),
        grid=(2,),
        in_specs=[vec, vec, vec,
                  pl.BlockSpec((1, SSM_HALF, ns), lambda j: (j, 0, 0)),
                  pl.BlockSpec((1, SSM_HALF, ns), lambda j: (j, 0, 0)),
                  pl.BlockSpec((1, ns, SSM_HALF), lambda j: (j, 0, 0)),
                  pl.BlockSpec((1, ns, SSM_HALF), lambda j: (j, 0, 0))],
        out_specs=(pl.BlockSpec((1, SSM_BLOCK * SSM_HALF, 2 * ns), lambda j: (j, 0, 0)),
                   pl.BlockSpec((1, 2 * ns, SSM_HALF), lambda j: (j, 0, 0)),
                   pl.BlockSpec((1, 8, ns), lambda j: (j, 0, 0))),
        name="ssm_discretise",
    )(row(a_re), row(a_im), row(ldt), ebr, ebi, ecr, eci)


def _ssm_kernel(s0_ref, s1_ref, s2_ref, s3_ref, eb_ref, ec_ref, abar_ref, d_ref, wglu_ref, bglu_ref,
                y0_ref, y1_ref, y2_ref, y3_ref, h_ref, carry_ref):
    nb, nr, nt, ns = BATCH, SSM_ROWS, SSM_CHUNK, SSM_HALF_STATE
    rows = nb * nr
    s_refs = (s0_ref, s1_ref, s2_ref, s3_ref)
    y_refs = (y0_ref, y1_ref, y2_ref, y3_ref)

    @pl.when(pl.program_id(0) == 0)
    def _():
        carry_ref[...] = jnp.zeros_like(carry_ref)

    def s_tile(t, lo, width):
        step = pl.ds(t, nr, stride=nt)
        slabs = [s_refs[k][:, step, :].reshape(rows, LANES)
                 for k in range(lo // LANES, (lo + width) // LANES)]
        return jnp.concatenate(slabs, axis=1)

    ntile = ns // LANES

    strip = 2 * LANES

    def advance(j, s_j, nsteps):
        mul_row = {1: 0, SSM_BLOCK: 4}[nsteps]
        w_rows = slice((SSM_BLOCK - nsteps) * SSM_HALF, SSM_BLOCK * SSM_HALF)
        tiles = [None] * (2 * ntile)
        for k in range(ntile):
            lo = k * LANES
            bu = jnp.dot(s_j, eb_ref[j, w_rows, k * strip:(k + 1) * strip], preferred_element_type=F32)
            ar = abar_ref[j, mul_row:mul_row + 1, lo:lo + LANES]
            ai = abar_ref[j, mul_row + 1:mul_row + 2, lo:lo + LANES]
            hr = h_ref[j, 2 * k]
            hi = h_ref[j, 2 * k + 1]
            tiles[2 * k] = ar * hr - ai * hi + bu[:, 0:LANES]
            tiles[2 * k + 1] = ar * hi + ai * hr + bu[:, LANES:strip]
            h_ref[j, 2 * k] = tiles[2 * k]
            h_ref[j, 2 * k + 1] = tiles[2 * k + 1]
        return tiles

    h_ref[...] = jnp.zeros_like(h_ref)

    def local_block(q, carry):
        for j in range(2):
            s_j = jnp.concatenate([s_tile(q * SSM_BLOCK + i, j * SSM_HALF, SSM_HALF)
                                   for i in range(SSM_BLOCK)], axis=1).astype(BF16)
            advance(j, s_j, SSM_BLOCK)
        return carry

    lax.fori_loop(0, nt // SSM_BLOCK, local_block, 0)

    for j in range(2):
        for k in range(ntile):
            lo = k * LANES
            qr = abar_ref[j, 2:3, lo:lo + LANES]
            qi = abar_ref[j, 3:4, lo:lo + LANES]
            st_r = carry_ref[j, 2 * k]
            st_i = carry_ref[j, 2 * k + 1]
            for c in range(nr):
                sel = pl.ds(c, nb, stride=nr)
                loc_r = h_ref[j, 2 * k, sel, :]
                loc_i = h_ref[j, 2 * k + 1, sel, :]
                h_ref[j, 2 * k, sel, :] = st_r
                h_ref[j, 2 * k + 1, sel, :] = st_i
                st_r, st_i = qr * st_r - qi * st_i + loc_r, qr * st_i + qi * st_r + loc_i
            carry_ref[j, 2 * k] = st_r
            carry_ref[j, 2 * k + 1] = st_i

    def output_step(t, carry):
        s_t = s_tile(t, 0, SSM_WIDTH)
        ys = []
        for j in range(2):
            tiles = advance(j, s_t[:, j * SSM_HALF:(j + 1) * SSM_HALF].astype(BF16), 1)
            ys.append(jnp.dot(jnp.concatenate(tiles, axis=1).astype(BF16), ec_ref[j],
                              preferred_element_type=F32))
        y = jnp.concatenate(ys, axis=1) + d_ref[...] * s_t
        y = jax.nn.gelu(y)
        z = jnp.dot(y.astype(BF16), wglu_ref[...], preferred_element_type=F32) + bglu_ref[...]
        out = y * jax.nn.sigmoid(z)
        step = pl.ds(t, nr, stride=nt)
        for k in range(SSM_WIDTH // LANES):
            y_refs[k][:, step, :] = out[:, k * LANES:(k + 1) * LANES].reshape(nb, nr, LANES)
        return carry

    lax.fori_loop(0, nt, output_step, 0)


def _ssm(s, eb, ec, abar, d_skip, w_glu_b, b_glu):
    nt, nr, ns = SSM_CHUNK, SSM_ROWS, SSM_HALF_STATE
    nslab = SSM_WIDTH // LANES
    const3 = lambda i: (0, 0, 0)
    const2 = lambda i: (0, 0)
    once = dict(pipeline_mode=pl.Buffered(1))
    slab = lambda k: pl.BlockSpec((BATCH, nr * nt, LANES), lambda i: (0, i, k))
    return pl.pallas_call(
        _ssm_kernel,
        out_shape=tuple(jax.ShapeDtypeStruct((BATCH, SEQ, LANES), F32) for _ in range(nslab)),
        grid=(SEQ // (nr * nt),),
        in_specs=[slab(k) for k in range(nslab)] + [
            pl.BlockSpec((2, SSM_BLOCK * SSM_HALF, 2 * ns), const3, **once),
            pl.BlockSpec((2, 2 * ns, SSM_HALF), const3, **once),
            pl.BlockSpec((2, 8, ns), const3),
            pl.BlockSpec((1, SSM_WIDTH), const2),
            pl.BlockSpec((SSM_WIDTH, SSM_WIDTH), const2),
            pl.BlockSpec((1, SSM_WIDTH), const2)],
        out_specs=tuple(slab(0) for _ in range(nslab)),
        scratch_shapes=[pltpu.VMEM((2, 2 * ns // LANES, BATCH * nr, LANES), F32),
                        pltpu.VMEM((2, 2 * ns // LANES, BATCH, LANES), F32)],
        compiler_params=pltpu.CompilerParams(dimension_semantics=("arbitrary",),
                                             vmem_limit_bytes=VMEM_LIMIT_BYTES),
        name="s5_ssm_glu",
    )(s, s, s, s, eb, ec, abar, d_skip.reshape(1, SSM_WIDTH), w_glu_b, b_glu.reshape(1, SSM_WIDTH))


def _ffn_kernel(x_ref, o_ref, y0_ref, y1_ref, y2_ref, y3_ref, g1_ref, sh2_ref, sc2_ref, g2_ref, wo_ref,
                ln1g_ref, ln1b_ref, wup_ref, cw_ref, wdn_ref, ln2g_ref, ln2b_ref, out_ref, carry_ref):
    tm = FFN_ROWS

    @pl.when(pl.program_id(1) == 0)
    def _():
        carry_ref[...] = jnp.zeros_like(carry_ref)

    y = jnp.concatenate([y0_ref[0], y1_ref[0], y2_ref[0], y3_ref[0]], axis=1).astype(BF16)
    m = (jnp.dot(o_ref[0], wo_ref[0:ATTN_WIDTH, :], preferred_element_type=F32)
         + jnp.dot(y, wo_ref[ATTN_WIDTH:, :], preferred_element_type=F32))
    x1 = _layernorm(DEEPNORM_ALPHA * x_ref[0] + g1_ref[0] * m) * ln1g_ref[...] + ln1b_ref[...]
    ub = (_layernorm(x1) * (1.0 + sc2_ref[0]) + sh2_ref[0]).astype(BF16)

    halo = [carry_ref[idx] for idx in range(2 * FFN_STEPS)]
    tails = [None] * (2 * FFN_STEPS)

    def up(idx):
        return jnp.dot(ub, wup_ref[idx], preferred_element_type=F32)

    def conv_cols(idx, hcur):
        ext = jnp.concatenate([halo[idx], hcur], axis=0)
        tails[idx] = hcur[tm - SUBLANES:tm, :]
        w = cw_ref[idx]
        return (w[2:3] * hcur + w[1:2] * ext[SUBLANES - 1:SUBLANES - 1 + tm, :]
                + w[0:1] * ext[SUBLANES - 2:SUBLANES - 2 + tm, :] + w[3:4])

    f = None
    group = []
    nxt = (up(0), up(FFN_STEPS))
    for j in range(FFN_STEPS):
        cur = nxt
        if j + 1 < FFN_STEPS:
            nxt = (up(j + 1), up(j + 1 + FFN_STEPS))
        val = conv_cols(j, cur[0])
        gate = conv_cols(j + FFN_STEPS, cur[1])
        group.append((gate * jax.nn.sigmoid(gate) * val).astype(BF16))
        if len(group) < FFN_GROUP and j + 1 < FFN_STEPS:
            continue
        lo = (j + 1 - len(group)) * FFN_COLS
        a = jnp.concatenate(group, axis=1) if len(group) > 1 else group[0]
        part = jnp.dot(a, wdn_ref[lo:(j + 1) * FFN_COLS, :], preferred_element_type=F32)
        f = part if f is None else f + part
        group = []
    for idx in range(2 * FFN_STEPS):
        carry_ref[idx] = tails[idx]
    out_ref[0] = (_layernorm(DEEPNORM_ALPHA * x1 + g2_ref[0] * f) * ln2g_ref[...] + ln2b_ref[...])


def _outproj_ffn(x, o, y_slabs, mod3, w_out_b, ln1_g, ln1_b, w_up_c, conv_c, w_down_c, ln2_g, ln2_b):
    tm = FFN_ROWS
    row = lambda b, i: (b, i, 0)
    modv = lambda k: pl.BlockSpec((1, 1, D_MODEL), lambda b, i: (b, 0, k))
    c2 = lambda b, i: (0, 0)
    c3 = lambda b, i: (0, 0, 0)
    vec = pl.BlockSpec((1, D_MODEL), c2)
    return pl.pallas_call(
        _ffn_kernel,
        out_shape=jax.ShapeDtypeStruct((BATCH, SEQ, D_MODEL), F32),
        grid=(BATCH, SEQ // tm),
        in_specs=[pl.BlockSpec((1, tm, D_MODEL), row),
                  pl.BlockSpec((1, tm, ATTN_WIDTH), row),
                  *[pl.BlockSpec((1, tm, LANES), row) for _ in range(SSM_WIDTH // LANES)],
                  modv(2), modv(3), modv(4), modv(5),
                  pl.BlockSpec((D_MODEL, D_MODEL), c2, pipeline_mode=pl.Buffered(1)),
                  vec, vec,
                  pl.BlockSpec((2 * FFN_STEPS, D_MODEL, FFN_COLS), c3, pipeline_mode=pl.Buffered(1)),
                  pl.BlockSpec((2 * FFN_STEPS, SUBLANES, FFN_COLS), c3),
                  pl.BlockSpec((D_FF, D_MODEL), c2, pipeline_mode=pl.Buffered(1)),
                  vec, vec],
        out_specs=pl.BlockSpec((1, tm, D_MODEL), row),
        scratch_shapes=[pltpu.VMEM((2 * FFN_STEPS, SUBLANES, FFN_COLS), F32)],
        compiler_params=pltpu.CompilerParams(dimension_semantics=("arbitrary", "arbitrary"),
                                             vmem_limit_bytes=VMEM_LIMIT_BYTES),
        name="outproj_ffn",
    )(x, o, *y_slabs, mod3, mod3, mod3, mod3, w_out_b, ln1_g, ln1_b, w_up_c, conv_c, w_down_c, ln2_g, ln2_b)


def kernel(x, c, w_ada, b_ada, w_in, lam_q1, lam_k1, lam_q2, lam_k2, subln_g, ssm_a_re, ssm_a_im,
           ssm_b_re, ssm_b_im, ssm_c_re, ssm_c_im, ssm_d, ssm_log_dt, w_glu, b_glu, w_out, ln1_g,
           ln1_b, w_up, conv_w, conv_b, w_down, ln2_g, ln2_b, rel_bias):
    bias_tiles = _bias_tiles(rel_bias)
    for layer in range(DEPTH):
        lam_init = 0.8 - 0.6 * math.exp(-0.3 * layer)
        mod3 = _modulation(c, w_ada[layer], b_ada[layer]).reshape(BATCH, 1, 6 * D_MODEL)
        qt, k, vt, s = _in_projection(x, mod3, w_in[layer].astype(BF16))
        o = _attention(qt, k, vt, bias_tiles, lam_q1[layer][None], lam_k1[layer][None],
                       lam_q2[layer][None], lam_k2[layer][None], subln_g[layer][:, None], lam_init)
        eb, ec, abar = _ssm_prep(ssm_a_re[layer], ssm_a_im[layer], ssm_b_re[layer], ssm_b_im[layer],
                                 ssm_c_re[layer], ssm_c_im[layer], ssm_log_dt[layer])
        y = _ssm(s, eb, ec, abar, ssm_d[layer], w_glu[layer].astype(BF16), b_glu[layer])
        w_up_c = jnp.transpose(w_up[layer].astype(BF16).reshape(D_MODEL, 2 * FFN_STEPS, FFN_COLS),
                               (1, 0, 2))
        conv_c = jnp.concatenate([conv_w[layer], conv_b[layer][None],
                                  jnp.zeros((SUBLANES - 4, 2 * D_FF), F32)], axis=0)
        conv_c = jnp.transpose(conv_c.reshape(SUBLANES, 2 * FFN_STEPS, FFN_COLS), (1, 0, 2))
        x = _outproj_ffn(x, o, y, mod3, w_out[layer].astype(BF16), ln1_g[layer][None],
                         ln1_b[layer][None], w_up_c, conv_c, w_down[layer].astype(BF16),
                         ln2_g[layer][None],
                         ln2_b[layer][None])
    return x
```

```python
import functools
import math

import numpy as np
import jax
import jax.numpy as jnp
from jax import lax
from jax.experimental import pallas as pl
from jax.experimental.pallas import tpu as pltpu

F32 = jnp.float32
BF16 = jnp.bfloat16

D_MODEL = 1024
BATCH = 16
SEQ = 4096
ATTN_WIDTH = 512
SSM_WIDTH = 512
ATTN_HEADS = 4
HEAD_DIM = 64
HEAD_COLS = 2 * HEAD_DIM
SSM_GROUP = 16
SSM_GROUPS = 32
SSM_STATE = 64
D_FF = 2816
REL_BUCKETS = 32
REL_MAX_DIST = 128
LN_EPS = 1e-5
NEG_INF = -1e30
DEPTH = 1
DEEPNORM_ALPHA = (2 * DEPTH) ** 0.25

LANES = 128
SUBLANES = 8
MXU_DIM = 256
VMEM_LIMIT_BYTES = 56 * 1024 * 1024

PROJ_ROWS = 512
ATTN_TQ = 512
ATTN_TK = 512
V_ROWS = HEAD_COLS + 16
LOG2E = math.log2(math.e)
SSM_CHUNK = 16
SSM_ROWS = 16
SSM_BLOCK = 4
SSM_HALF = SSM_WIDTH // 2
SSM_HALF_STATE = (SSM_GROUPS // 2) * SSM_STATE
FFN_ROWS = 512
FFN_COLS = 256
FFN_STEPS = D_FF // FFN_COLS
FFN_GROUP = FFN_STEPS


def _layernorm(x):
    mu = jnp.mean(x, axis=-1, keepdims=True)
    xc = x - mu
    var = jnp.mean(xc * xc, axis=-1, keepdims=True)
    return xc * lax.rsqrt(var + LN_EPS)


def _mod_kernel(c_ref, w_ref, b_ref, o_ref):
    c = c_ref[...]
    act = c * jax.nn.sigmoid(c)
    o_ref[...] = jnp.dot(act, w_ref[...], precision=lax.Precision.HIGHEST,
                         preferred_element_type=F32) + b_ref[...]


def _modulation(c, w_ada, b_ada):
    n = w_ada.shape[1]
    blk = D_MODEL
    return pl.pallas_call(
        _mod_kernel,
        out_shape=jax.ShapeDtypeStruct((BATCH, n), F32),
        grid=(n // blk,),
        in_specs=[pl.BlockSpec((BATCH, D_MODEL), lambda i: (0, 0)),
                  pl.BlockSpec((D_MODEL, blk), lambda i: (0, i)),
                  pl.BlockSpec((1, blk), lambda i: (0, i))],
        out_specs=pl.BlockSpec((BATCH, blk), lambda i: (0, i)),
        name="adaln_mod",
    )(c, w_ada, b_ada.reshape(1, n))


def _inproj_kernel(x_ref, sh_ref, sc_ref, wqt_ref, wk_ref, wvt_ref, ws_ref,
                   qt_ref, k_ref, vt_ref, s_ref):
    nt = (((1,), (1,)), ((), ()))
    hr = x_ref.shape[1] // 2
    outs = []
    for hf in range(2):
        r = slice(hf * hr, (hf + 1) * hr)
        u = _layernorm(x_ref[0, r, :]) * (1.0 + sc_ref[0]) + sh_ref[0]
        ub = u.astype(BF16)
        qt = lax.dot_general(wqt_ref[...], ub, nt, preferred_element_type=F32)
        kk = jnp.dot(ub, wk_ref[...], preferred_element_type=F32)
        vt = lax.dot_general(wvt_ref[...], ub, nt, preferred_element_type=F32).astype(BF16)
        ss = jnp.dot(ub, ws_ref[...], preferred_element_type=F32)
        outs.append((qt, kk, vt, ss))
    for hf in range(2):
        r = slice(hf * hr, (hf + 1) * hr)
        qt, kk, vt, ss = outs[hf]
        qt_ref[0, :, r] = (qt * (HEAD_DIM ** -0.5 * LOG2E)).astype(BF16)
        k_ref[0, r, :] = kk.astype(BF16)
        ones = jnp.ones((V_ROWS - HEAD_COLS, hr), BF16)
        for h in range(ATTN_HEADS):
            vt_ref[0, h, 0:HEAD_COLS, r] = vt[h * HEAD_COLS:(h + 1) * HEAD_COLS, :]
            vt_ref[0, h, HEAD_COLS:V_ROWS, r] = ones
        s_ref[0, r, :] = ss


def _in_projection(x, mod3, w_in_b):
    tm = PROJ_ROWS
    aw = ATTN_WIDTH
    wqt = jnp.transpose(w_in_b[:, 0:aw])
    wk = w_in_b[:, aw:2 * aw]
    wvt = jnp.transpose(w_in_b[:, 2 * aw:3 * aw])
    ws = w_in_b[:, 3 * aw:]
    const = lambda b, i: (0, 0)
    tposed = pl.BlockSpec((1, aw, tm), lambda b, i: (b, 0, i))
    rowblk = lambda w: pl.BlockSpec((1, tm, w), lambda b, i: (b, i, 0))
    return pl.pallas_call(
        _inproj_kernel,
        out_shape=(jax.ShapeDtypeStruct((BATCH, aw, SEQ), BF16),
                   jax.ShapeDtypeStruct((BATCH, SEQ, aw), BF16),
                   jax.ShapeDtypeStruct((BATCH, ATTN_HEADS, V_ROWS, SEQ), BF16),
                   jax.ShapeDtypeStruct((BATCH, SEQ, SSM_WIDTH), F32)),
        grid=(BATCH, SEQ // tm),
        in_specs=[rowblk(D_MODEL),
                  pl.BlockSpec((1, 1, D_MODEL), lambda b, i: (b, 0, 0)),
                  pl.BlockSpec((1, 1, D_MODEL), lambda b, i: (b, 0, 1)),
                  pl.BlockSpec((aw, D_MODEL), const), pl.BlockSpec((D_MODEL, aw), const),
                  pl.BlockSpec((aw, D_MODEL), const), pl.BlockSpec((D_MODEL, SSM_WIDTH), const)],
        out_specs=(tposed, rowblk(aw),
                   pl.BlockSpec((1, ATTN_HEADS, V_ROWS, tm), lambda b, i: (b, 0, 0, i)),
                   rowblk(SSM_WIDTH)),
        compiler_params=pltpu.CompilerParams(vmem_limit_bytes=VMEM_LIMIT_BYTES),
        name="ln_inproj",
    )(x, mod3, mod3, wqt, wk, wvt, ws)


def _t5_bucket_np(dist):
    max_exact = REL_BUCKETS // 2
    distf = np.maximum(dist, 1).astype(np.float32)
    large = max_exact + (np.log(distf / max_exact) / math.log(REL_MAX_DIST / max_exact)
                         * (REL_BUCKETS - max_exact)).astype(np.int32)
    large = np.minimum(large, REL_BUCKETS - 1)
    return np.where(dist < max_exact, dist, large).astype(np.int32)


def _bucket_tiles():
    kpos = np.arange(ATTN_TK)[:, None]
    qpos = np.arange(ATTN_TQ)[None, :]
    diag = qpos - kpos
    diag_b = np.where(diag >= 0, _t5_bucket_np(np.maximum(diag, 0)), -1)
    sub_b = _t5_bucket_np(diag + ATTN_TK)
    return np.stack([diag_b, sub_b]).astype(np.int32)


def _bias_kernel(tab_ref, bucket_ref, o_ref):
    h = pl.program_id(0)
    far = tab_ref[REL_BUCKETS - 1, h]
    for tile in range(2):
        b = bucket_ref[tile]
        acc = jnp.full(b.shape, NEG_INF, F32)
        for n in range(REL_BUCKETS):
            acc = jnp.where(b == n, (tab_ref[n, h] - far) * LOG2E, acc)
        o_ref[0, tile] = acc


def _bias_tiles(rel_bias):
    buckets = jnp.asarray(_bucket_tiles())
    return pl.pallas_call(
        _bias_kernel,
        out_shape=jax.ShapeDtypeStruct((ATTN_HEADS, 2, ATTN_TK, ATTN_TQ), F32),
        grid=(ATTN_HEADS,),
        in_specs=[pl.BlockSpec(memory_space=pltpu.SMEM),
                  pl.BlockSpec((2, ATTN_TK, ATTN_TQ), lambda h: (0, 0, 0))],
        out_specs=pl.BlockSpec((1, 2, ATTN_TK, ATTN_TQ), lambda h: (h, 0, 0, 0)),
        name="t5_bias_tiles",
    )(rel_bias, buckets)


def _attn_kernel(qt_ref, k_ref, vt_ref, bias_ref, lq1_ref, lk1_ref, lq2_ref, lk2_ref, g_ref,
                 o_ref, m_ref, acc_ref, qs_ref, *, lam_init):
    qi = pl.program_id(2)
    tq, tk = ATTN_TQ, ATTN_TK

    qt = qt_ref[0].astype(F32)
    row = lax.broadcasted_iota(jnp.int32, qt.shape, 0)
    qs_ref[0] = jnp.where(row < HEAD_DIM, qt, 0.0).astype(BF16)
    qs_ref[1] = jnp.where(row >= HEAD_DIM, qt, 0.0).astype(BF16)

    def scores(j):
        k = k_ref[0, pl.ds(pl.multiple_of(j * tk, tk), tk), :]
        return [jnp.dot(k, qs_ref[c], preferred_element_type=F32) for c in range(2)]

    def update(j, ss):
        vt = vt_ref[0, 0, :, pl.ds(pl.multiple_of(j * tk, tk), tk)]
        for c in range(2):
            s = ss[c]
            m_prev = m_ref[c]
            m_new = jnp.maximum(m_prev, jnp.max(s, axis=0, keepdims=True))
            alpha = jnp.exp2(m_prev - m_new)
            p = jnp.exp2(s - m_new).astype(BF16)
            acc_ref[c] = alpha * acc_ref[c] + jnp.dot(vt, p, preferred_element_type=F32)
            m_ref[c] = m_new

    m_ref[...] = jnp.full(m_ref.shape, NEG_INF, F32)
    acc_ref[...] = jnp.zeros_like(acc_ref)

    nfar = jnp.maximum(qi - 1, 0)

    def far_pair(i, carry):
        s_a = scores(2 * i)
        s_b = scores(2 * i + 1)
        update(2 * i, s_a)
        update(2 * i + 1, s_b)
        return carry

    lax.fori_loop(0, nfar // 2, far_pair, 0)

    half = tk // 2

    def diag_scores():
        parts = []
        for part in range(2):
            koff = pl.multiple_of(qi * tk + part * half, half)
            k = k_ref[0, pl.ds(koff, half), :]
            q0 = part * half
            parts.append([jnp.dot(k, qs_ref[c, :, q0:tq], preferred_element_type=F32)
                          + bias_ref[0, 0, part * half:(part + 1) * half, q0:tq] for c in range(2)])
        return parts

    def diag_update(parts):
        for part in range(2):
            koff = pl.multiple_of(qi * tk + part * half, half)
            vt = vt_ref[0, 0, :, pl.ds(koff, half)]
            q0 = part * half
            for c in range(2):
                s = parts[part][c]
                m_prev = m_ref[c, :, q0:tq]
                m_new = jnp.maximum(m_prev, jnp.max(s, axis=0, keepdims=True))
                alpha = jnp.exp2(m_prev - m_new)
                p = jnp.exp2(s - m_new).astype(BF16)
                acc_ref[c, :, q0:tq] = (alpha * acc_ref[c, :, q0:tq]
                                        + jnp.dot(vt, p, preferred_element_type=F32))
                m_ref[c, :, q0:tq] = m_new

    def sub_and_diag(i, carry):
        s_sub = [s + bias_ref[0, 1] for s in scores(qi - 1)]
        parts = diag_scores()
        update(qi - 1, s_sub)
        diag_update(parts)
        return carry

    def far_sub_and_diag(i, carry):
        s_far = scores(nfar - 1)
        s_sub = [s + bias_ref[0, 1] for s in scores(qi - 1)]
        parts = diag_scores()
        update(nfar - 1, s_far)
        update(qi - 1, s_sub)
        diag_update(parts)
        return carry

    odd = nfar % 2
    lax.fori_loop(0, odd, far_sub_and_diag, 0)
    lax.fori_loop(0, jnp.minimum(qi, 1) * (1 - odd), sub_and_diag, 0)

    def diag_only(i, carry):
        diag_update(diag_scores())
        return carry

    lax.fori_loop(0, 1 - jnp.minimum(qi, 1), diag_only, 0)

    lam = (jnp.exp(jnp.sum(lq1_ref[...] * lk1_ref[...], axis=1, keepdims=True))
           - jnp.exp(jnp.sum(lq2_ref[...] * lk2_ref[...], axis=1, keepdims=True)) + lam_init)
    on = [acc_ref[c, 0:HEAD_COLS, :] / acc_ref[c, HEAD_COLS:HEAD_COLS + 1, :] for c in range(2)]
    ot = on[0] - lam * on[1]
    ms = jnp.mean(ot * ot, axis=0, keepdims=True)
    ot = ot * lax.rsqrt(ms + LN_EPS) * g_ref[...] * (1.0 - lam_init)
    o_ref[0] = jnp.transpose(ot).astype(BF16)


def _attention(qt, k, vt, bias_tiles, lam_q1, lam_k1, lam_q2, lam_k2, subln_g, lam_init):
    tq = ATTN_TQ
    small = lambda b, h, i: (0, 0)
    return pl.pallas_call(
        functools.partial(_attn_kernel, lam_init=lam_init),
        out_shape=jax.ShapeDtypeStruct((BATCH, SEQ, ATTN_WIDTH), BF16),
        grid=(BATCH, ATTN_HEADS, SEQ // tq),
        in_specs=[pl.BlockSpec((1, HEAD_COLS, tq), lambda b, h, i: (b, h, i)),
                  pl.BlockSpec((1, SEQ, HEAD_COLS), lambda b, h, i: (b, 0, h)),
                  pl.BlockSpec((1, 1, V_ROWS, SEQ), lambda b, h, i: (b, h, 0, 0)),
                  pl.BlockSpec((1, 2, ATTN_TK, tq), lambda b, h, i: (h, 0, 0, 0)),
                  pl.BlockSpec((1, HEAD_DIM), small), pl.BlockSpec((1, HEAD_DIM), small),
                  pl.BlockSpec((1, HEAD_DIM), small), pl.BlockSpec((1, HEAD_DIM), small),
                  pl.BlockSpec((HEAD_COLS, 1), small)],
        out_specs=pl.BlockSpec((1, tq, HEAD_COLS), lambda b, h, i: (b, i, h)),
        scratch_shapes=[pltpu.VMEM((2, 1, tq), F32),
                        pltpu.VMEM((2, V_ROWS, tq), F32),
                        pltpu.VMEM((2, HEAD_COLS, tq), BF16)],
        compiler_params=pltpu.CompilerParams(vmem_limit_bytes=VMEM_LIMIT_BYTES),
        name="diff_attention",
    )(qt, k, vt, bias_tiles, lam_q1, lam_k1, lam_q2, lam_k2, subln_g)


def _ssm_prep_kernel(are_ref, aim_ref, ldt_ref, ebr_ref, ebi_ref, ecr_ref, eci_ref,
                     eb_ref, ec_ref, abar_ref):
    ar = are_ref[0]
    ai = aim_ref[0]
    dt = jnp.exp(ldt_ref[0])
    mag = jnp.exp(ar * dt)
    ang = ai * dt
    pr = mag * jnp.cos(ang)
    pim = mag * jnp.sin(ang)
    den = ar * ar + ai * ai
    fr = ((pr - 1.0) * ar + pim * ai) / den
    fi = (pim * ar - (pr - 1.0) * ai) / den
    ebr = ebr_ref[0]
    ebi = ebi_ref[0]
    ns = SSM_HALF_STATE
    bre = fr * ebr - fi * ebi
    bim = fr * ebi + fi * ebr
    cre = ecr_ref[0].astype(BF16)
    cim = (-eci_ref[0]).astype(BF16)
    pows = [(jnp.ones_like(pr), jnp.zeros_like(pr)), (pr, pim)]
    for _ in range(2, SSM_BLOCK + 1):
        xr, xi = pows[-1]
        pows.append((xr * pr - xi * pim, xr * pim + xi * pr))
    for i in range(SSM_BLOCK):
        xr, xi = pows[SSM_BLOCK - 1 - i]
        blk_r = (bre * xr - bim * xi).astype(BF16)
        blk_i = (bre * xi + bim * xr).astype(BF16)
        rows = slice(i * SSM_HALF, (i + 1) * SSM_HALF)
        for k in range(ns // LANES):
            src = slice(k * LANES, (k + 1) * LANES)
            eb_ref[0, rows, 2 * k * LANES:(2 * k + 1) * LANES] = blk_r[:, src]
            eb_ref[0, rows, (2 * k + 1) * LANES:(2 * k + 2) * LANES] = blk_i[:, src]
    for k in range(ns // LANES):
        src = slice(k * LANES, (k + 1) * LANES)
        ec_ref[0, 2 * k * LANES:(2 * k + 1) * LANES, :] = cre[src, :]
        ec_ref[0, (2 * k + 1) * LANES:(2 * k + 2) * LANES, :] = cim[src, :]
    qr, qi = pr, pim
    for _ in range(int(math.log2(SSM_CHUNK))):
        qr, qi = qr * qr - qi * qi, 2.0 * qr * qi
    br, bi = pows[SSM_BLOCK]
    abar_ref[0] = jnp.concatenate([pr, pim, qr, qi, br, bi, jnp.zeros((2, ns), F32)], axis=0)


def _block_diag_half(w):
    g2 = SSM_GROUPS // 2
    w = w.reshape(2, g2, w.shape[1], w.shape[2])
    eye = jnp.eye(g2, dtype=w.dtype)
    out = w[:, :, :, None, :] * eye[None, :, None, :, None]
    return out.reshape(2, g2 * w.shape[2], g2 * w.shape[3])


def _ssm_prep(a_re, a_im, b_re, b_im, c_re, c_im, log_dt):
    ns = SSM_HALF_STATE
    row = lambda v: v.reshape(2, 1, ns)
    ldt = jnp.broadcast_to(log_dt[:, None], (SSM_GROUPS, SSM_STATE))
    ebr = _block_diag_half(jnp.transpose(b_re, (0, 2, 1)))
    ebi = _block_diag_half(jnp.transpose(b_im, (0, 2, 1)))
    ecr = _block_diag_half(jnp.transpose(c_re, (0, 2, 1)))
    eci = _block_diag_half(jnp.transpose(c_im, (0, 2, 1)))
    vec = pl.BlockSpec((1, 1, ns), lambda j: (j, 0, 0))
    return pl.pallas_call(
        _ssm_prep_kernel,
        out_shape=(jax.ShapeDtypeStruct((2, SSM_BLOCK * SSM_HALF, 2 * ns), BF16),
                   jax.ShapeDtypeStruct((2, 2 * ns, SSM_HALF), BF16),
                   jax.ShapeDtypeStruct((2, 8, ns), F32)),
        grid=(2,),
        in_specs=[vec, vec, vec,
                  pl.BlockSpec((1, SSM_HALF, ns), lambda j: (j, 0, 0)),
                  pl.BlockSpec((1, SSM_HALF, ns), lambda j: (j, 0, 0)),
                  pl.BlockSpec((1, ns, SSM_HALF), lambda j: (j, 0, 0)),
                  pl.BlockSpec((1, ns, SSM_HALF), lambda j: (j, 0, 0))],
        out_specs=(pl.BlockSpec((1, SSM_BLOCK * SSM_HALF, 2 * ns), lambda j: (j, 0, 0)),
                   pl.BlockSpec((1, 2 * ns, SSM_HALF), lambda j: (j, 0, 0)),
                   pl.BlockSpec((1, 8, ns), lambda j: (j, 0, 0))),
        name="ssm_discretise",
    )(row(a_re), row(a_im), row(ldt), ebr, ebi, ecr, eci)


def _ssm_kernel(s0_ref, s1_ref, s2_ref, s3_ref, eb_ref, ec_ref, abar_ref, d_ref, wglu_ref, bglu_ref,
                y0_ref, y1_ref, y2_ref, y3_ref, h_ref, carry_ref):
    nb, nr, nt, ns = BATCH, SSM_ROWS, SSM_CHUNK, SSM_HALF_STATE
    rows = nb * nr
    s_refs = (s0_ref, s1_ref, s2_ref, s3_ref)
    y_refs = (y0_ref, y1_ref, y2_ref, y3_ref)

    @pl.when(pl.program_id(0) == 0)
    def _():
        carry_ref[...] = jnp.zeros_like(carry_ref)

    def s_tile(t, lo, width):
        step = pl.ds(t, nr, stride=nt)
        slabs = [s_refs[k][:, step, :].reshape(rows, LANES)
                 for k in range(lo // LANES, (lo + width) // LANES)]
        return jnp.concatenate(slabs, axis=1)

    ntile = ns // LANES

    strip = 2 * LANES

    def advance(j, s_j, nsteps):
        mul_row = {1: 0, SSM_BLOCK: 4}[nsteps]
        w_rows = slice((SSM_BLOCK - nsteps) * SSM_HALF, SSM_BLOCK * SSM_HALF)
        tiles = [None] * (2 * ntile)
        for k in range(ntile):
            lo = k * LANES
            bu = jnp.dot(s_j, eb_ref[j, w_rows, k * strip:(k + 1) * strip], preferred_element_type=F32)
            ar = abar_ref[j, mul_row:mul_row + 1, lo:lo + LANES]
            ai = abar_ref[j, mul_row + 1:mul_row + 2, lo:lo + LANES]
            hr = h_ref[j, 2 * k]
            hi = h_ref[j, 2 * k + 1]
            tiles[2 * k] = ar * hr - ai * hi + bu[:, 0:LANES]
            tiles[2 * k + 1] = ar * hi + ai * hr + bu[:, LANES:strip]
            h_ref[j, 2 * k] = tiles[2 * k]
            h_ref[j, 2 * k + 1] = tiles[2 * k + 1]
        return tiles

    h_ref[...] = jnp.zeros_like(h_ref)

    def local_block(q, carry):
        for j in range(2):
            s_j = jnp.concatenate([s_tile(q * SSM_BLOCK + i, j * SSM_HALF, SSM_HALF)
                                   for i in range(SSM_BLOCK)], axis=1).astype(BF16)
            advance(j, s_j, SSM_BLOCK)
        return carry

    lax.fori_loop(0, nt // SSM_BLOCK, local_block, 0)

    for j in range(2):
        for k in range(ntile):
            lo = k * LANES
            qr = abar_ref[j, 2:3, lo:lo + LANES]
            qi = abar_ref[j, 3:4, lo:lo + LANES]
            st_r = carry_ref[j, 2 * k]
            st_i = carry_ref[j, 2 * k + 1]
            for c in range(nr):
                sel = pl.ds(c, nb, stride=nr)
                loc_r = h_ref[j, 2 * k, sel, :]
                loc_i = h_ref[j, 2 * k + 1, sel, :]
                h_ref[j, 2 * k, sel, :] = st_r
                h_ref[j, 2 * k + 1, sel, :] = st_i
                st_r, st_i = qr * st_r - qi * st_i + loc_r, qr * st_i + qi * st_r + loc_i
            carry_ref[j, 2 * k] = st_r
            carry_ref[j, 2 * k + 1] = st_i

    def output_step(t, carry):
        s_t = s_tile(t, 0, SSM_WIDTH)
        ys = []
        for j in range(2):
            tiles = advance(j, s_t[:, j * SSM_HALF:(j + 1) * SSM_HALF].astype(BF16), 1)
            ys.append(jnp.dot(jnp.concatenate(tiles, axis=1).astype(BF16), ec_ref[j],
                              preferred_element_type=F32))
        y = jnp.concatenate(ys, axis=1) + d_ref[...] * s_t
        y = jax.nn.gelu(y)
        z = jnp.dot(y.astype(BF16), wglu_ref[...], preferred_element_type=F32) + bglu_ref[...]
        out = y * jax.nn.sigmoid(z)
        step = pl.ds(t, nr, stride=nt)
        for k in range(SSM_WIDTH // LANES):
            y_refs[k][:, step, :] = out[:, k * LANES:(k + 1) * LANES].reshape(nb, nr, LANES)
        return carry

    lax.fori_loop(0, nt, output_step, 0)


def _ssm(s, eb, ec, abar, d_skip, w_glu_b, b_glu):
    nt, nr, ns = SSM_CHUNK, SSM_ROWS, SSM_HALF_STATE
    nslab = SSM_WIDTH // LANES
    const3 = lambda i: (0, 0, 0)
    const2 = lambda i: (0, 0)
    once = dict(pipeline_mode=pl.Buffered(1))
    slab = lambda k: pl.BlockSpec((BATCH, nr * nt, LANES), lambda i: (0, i, k))
    return pl.pallas_call(
        _ssm_kernel,
        out_shape=tuple(jax.ShapeDtypeStruct((BATCH, SEQ, LANES), F32) for _ in range(nslab)),
        grid=(SEQ // (nr * nt),),
        in_specs=[slab(k) for k in range(nslab)] + [
            pl.BlockSpec((2, SSM_BLOCK * SSM_HALF, 2 * ns), const3, **once),
            pl.BlockSpec((2, 2 * ns, SSM_HALF), const3, **once),
            pl.BlockSpec((2, 8, ns), const3),
            pl.BlockSpec((1, SSM_WIDTH), const2),
            pl.BlockSpec((SSM_WIDTH, SSM_WIDTH), const2),
            pl.BlockSpec((1, SSM_WIDTH), const2)],
        out_specs=tuple(slab(0) for _ in range(nslab)),
        scratch_shapes=[pltpu.VMEM((2, 2 * ns // LANES, BATCH * nr, LANES), F32),
                        pltpu.VMEM((2, 2 * ns // LANES, BATCH, LANES), F32)],
        compiler_params=pltpu.CompilerParams(dimension_semantics=("arbitrary",),
                                             vmem_limit_bytes=VMEM_LIMIT_BYTES),
        name="s5_ssm_glu",
    )(s, s, s, s, eb, ec, abar, d_skip.reshape(1, SSM_WIDTH), w_glu_b, b_glu.reshape(1, SSM_WIDTH))


def _ffn_kernel(x_ref, o_ref, y0_ref, y1_ref, y2_ref, y3_ref, g1_ref, sh2_ref, sc2_ref, g2_ref, wo_ref,
                ln1g_ref, ln1b_ref, wup_ref, cw_ref, wdn_ref, ln2g_ref, ln2b_ref, out_ref, carry_ref):
    tm = FFN_ROWS

    @pl.when(pl.program_id(1) == 0)
    def _():
        carry_ref[...] = jnp.zeros_like(carry_ref)

    y = jnp.concatenate([y0_ref[0], y1_ref[0], y2_ref[0], y3_ref[0]], axis=1).astype(BF16)
    m = (jnp.dot(o_ref[0], wo_ref[0:ATTN_WIDTH, :], preferred_element_type=F32)
         + jnp.dot(y, wo_ref[ATTN_WIDTH:, :], preferred_element_type=F32))
    x1 = _layernorm(DEEPNORM_ALPHA * x_ref[0] + g1_ref[0] * m) * ln1g_ref[...] + ln1b_ref[...]
    ub = (_layernorm(x1) * (1.0 + sc2_ref[0]) + sh2_ref[0]).astype(BF16)

    halo = [carry_ref[idx] for idx in range(2 * FFN_STEPS)]
    tails = [None] * (2 * FFN_STEPS)

    def up(idx):
        return jnp.dot(ub, wup_ref[idx], preferred_element_type=F32)

    def conv_cols(idx, hcur):
        ext = jnp.concatenate([halo[idx], hcur], axis=0)
        tails[idx] = hcur[tm - SUBLANES:tm, :]
        w = cw_ref[idx]
        return (w[2:3] * hcur + w[1:2] * ext[SUBLANES - 1:SUBLANES - 1 + tm, :]
                + w[0:1] * ext[SUBLANES - 2:SUBLANES - 2 + tm, :] + w[3:4])

    f = None
    group = []
    nxt = (up(0), up(FFN_STEPS))
    for j in range(FFN_STEPS):
        cur = nxt
        if j + 1 < FFN_STEPS:
            nxt = (up(j + 1), up(j + 1 + FFN_STEPS))
        val = conv_cols(j, cur[0])
        gate = conv_cols(j + FFN_STEPS, cur[1])
        group.append((gate * jax.nn.sigmoid(gate) * val).astype(BF16))
        if len(group) < FFN_GROUP and j + 1 < FFN_STEPS:
            continue
        lo = (j + 1 - len(group)) * FFN_COLS
        a = jnp.concatenate(group, axis=1) if len(group) > 1 else group[0]
        part = jnp.dot(a, wdn_ref[lo:(j + 1) * FFN_COLS, :], preferred_element_type=F32)
        f = part if f is None else f + part
        group = []
    for idx in range(2 * FFN_STEPS):
        carry_ref[idx] = tails[idx]
    out_ref[0] = (_layernorm(DEEPNORM_ALPHA * x1 + g2_ref[0] * f) * ln2g_ref[...] + ln2b_ref[...])


def _outproj_ffn(x, o, y_slabs, mod3, w_out_b, ln1_g, ln1_b, w_up_c, conv_c, w_down_c, ln2_g, ln2_b):
    tm = FFN_ROWS
    row = lambda b, i: (b, i, 0)
    modv = lambda k: pl.BlockSpec((1, 1, D_MODEL), lambda b, i: (b, 0, k))
    c2 = lambda b, i: (0, 0)
    c3 = lambda b, i: (0, 0, 0)
    vec = pl.BlockSpec((1, D_MODEL), c2)
    return pl.pallas_call(
        _ffn_kernel,
        out_shape=jax.ShapeDtypeStruct((BATCH, SEQ, D_MODEL), F32),
        grid=(BATCH, SEQ // tm),
        in_specs=[pl.BlockSpec((1, tm, D_MODEL), row),
                  pl.BlockSpec((1, tm, ATTN_WIDTH), row),
                  *[pl.BlockSpec((1, tm, LANES), row) for _ in range(SSM_WIDTH // LANES)],
                  modv(2), modv(3), modv(4), modv(5),
                  pl.BlockSpec((D_MODEL, D_MODEL), c2, pipeline_mode=pl.Buffered(1)),
                  vec, vec,
                  pl.BlockSpec((2 * FFN_STEPS, D_MODEL, FFN_COLS), c3, pipeline_mode=pl.Buffered(1)),
                  pl.BlockSpec((2 * FFN_STEPS, SUBLANES, FFN_COLS), c3),
                  pl.BlockSpec((D_FF, D_MODEL), c2, pipeline_mode=pl.Buffered(1)),
                  vec, vec],
        out_specs=pl.BlockSpec((1, tm, D_MODEL), row),
        scratch_shapes=[pltpu.VMEM((2 * FFN_STEPS, SUBLANES, FFN_COLS), F32)],
        compiler_params=pltpu.CompilerParams(dimension_semantics=("arbitrary", "arbitrary"),
                                             vmem_limit_bytes=VMEM_LIMIT_BYTES),
        name="outproj_ffn",
    )(x, o, *y_slabs, mod3, mod3, mod3, mod3, w_out_b, ln1_g, ln1_b, w_up_c, conv_c, w_down_c, ln2_g, ln2_b)


def kernel(x, c, w_ada, b_ada, w_in, lam_q1, lam_k1, lam_q2, lam_k2, subln_g, ssm_a_re, ssm_a_im,
           ssm_b_re, ssm_b_im, ssm_c_re, ssm_c_im, ssm_d, ssm_log_dt, w_glu, b_glu, w_out, ln1_g,
           ln1_b, w_up, conv_w, conv_b, w_down, ln2_g, ln2_b, rel_bias):
    bias_tiles = _bias_tiles(rel_bias)
    for layer in range(DEPTH):
        lam_init = 0.8 - 0.6 * math.exp(-0.3 * layer)
        mod3 = _modulation(c, w_ada[layer], b_ada[layer]).reshape(BATCH, 1, 6 * D_MODEL)
        qt, k, vt, s = _in_projection(x, mod3, w_in[layer].astype(BF16))
        o = _attention(qt, k, vt, bias_tiles, lam_q1[layer][None], lam_k1[layer][None],
                       lam_q2[layer][None], lam_k2[layer][None], subln_g[layer][:, None], lam_init)
        eb, ec, abar = _ssm_prep(ssm_a_re[layer], ssm_a_im[layer], ssm_b_re[layer], ssm_b_im[layer],
                                 ssm_c_re[layer], ssm_c_im[layer], ssm_log_dt[layer])
        y = _ssm(s, eb, ec, abar, ssm_d[layer], w_glu[layer].astype(BF16), b_glu[layer])
        w_up_c = jnp.transpose(w_up[layer].astype(BF16).reshape(D_MODEL, 2 * FFN_STEPS, FFN_COLS),
                               (1, 0, 2))
        conv_c = jnp.concatenate([conv_w[layer], conv_b[layer][None],
                                  jnp.zeros((SUBLANES - 4, 2 * D_FF), F32)], axis=0)
        conv_c = jnp.transpose(conv_c.reshape(SUBLANES, 2 * FFN_STEPS, FFN_COLS), (1, 0, 2))
        x = _outproj_ffn(x, o, y, mod3, w_out[layer].astype(BF16), ln1_g[layer][None],
                         ln1_b[layer][None], w_up_c, conv_c, w_down[layer].astype(BF16),
                         ln2_g[layer][None],
                         ln2_b[layer][None])
    return x
```

```python
import functools
import math

import numpy as np
import jax
import jax.numpy as jnp
from jax import lax
from jax.experimental import pallas as pl
from jax.experimental.pallas import tpu as pltpu

F32 = jnp.float32
BF16 = jnp.bfloat16

D_MODEL = 1024
BATCH = 16
SEQ = 4096
ATTN_WIDTH = 512
SSM_WIDTH = 512
ATTN_HEADS = 4
HEAD_DIM = 64
HEAD_COLS = 2 * HEAD_DIM
SSM_GROUP = 16
SSM_GROUPS = 32
SSM_STATE = 64
D_FF = 2816
REL_BUCKETS = 32
REL_MAX_DIST = 128
LN_EPS = 1e-5
NEG_INF = -1e30
DEPTH = 1
DEEPNORM_ALPHA = (2 * DEPTH) ** 0.25

LANES = 128
SUBLANES = 8
MXU_DIM = 256
VMEM_LIMIT_BYTES = 56 * 1024 * 1024

PROJ_ROWS = 512
ATTN_TQ = 512
ATTN_TK = 512
V_ROWS = HEAD_COLS + 16
LOG2E = math.log2(math.e)
SSM_CHUNK = 16
SSM_ROWS = 16
SSM_BLOCK = 4
SSM_HALF = SSM_WIDTH // 2
SSM_HALF_STATE = (SSM_GROUPS // 2) * SSM_STATE
FFN_ROWS = 512
FFN_COLS = 256
FFN_STEPS = D_FF // FFN_COLS
FFN_GROUP = FFN_STEPS


def _layernorm(x):
    mu = jnp.mean(x, axis=-1, keepdims=True)
    xc = x - mu
    var = jnp.mean(xc * xc, axis=-1, keepdims=True)
    return xc * lax.rsqrt(var + LN_EPS)


def _mod_kernel(c_ref, w_ref, b_ref, o_ref):
    c = c_ref[...]
    act = c * jax.nn.sigmoid(c)
    o_ref[...] = jnp.dot(act, w_ref[...], precision=lax.Precision.HIGHEST,
                         preferred_element_type=F32) + b_ref[...]


def _modulation(c, w_ada, b_ada):
    n = w_ada.shape[1]
    blk = D_MODEL
    return pl.pallas_call(
        _mod_kernel,
        out_shape=jax.ShapeDtypeStruct((BATCH, n), F32),
        grid=(n // blk,),
        in_specs=[pl.BlockSpec((BATCH, D_MODEL), lambda i: (0, 0)),
                  pl.BlockSpec((D_MODEL, blk), lambda i: (0, i)),
                  pl.BlockSpec((1, blk), lambda i: (0, i))],
        out_specs=pl.BlockSpec((BATCH, blk), lambda i: (0, i)),
        name="adaln_mod",
    )(c, w_ada, b_ada.reshape(1, n))


def _inproj_kernel(x_ref, sh_ref, sc_ref, wqt_ref, wk_ref, wvt_ref, ws_ref,
                   qt_ref, k_ref, vt_ref, s_ref):
    nt = (((1,), (1,)), ((), ()))
    hr = x_ref.shape[1] // 2
    outs = []
    for hf in range(2):
        r = slice(hf * hr, (hf + 1) * hr)
        u = _layernorm(x_ref[0, r, :]) * (1.0 + sc_ref[0]) + sh_ref[0]
        ub = u.astype(BF16)
        qt = lax.dot_general(wqt_ref[...], ub, nt, preferred_element_type=F32)
        kk = jnp.dot(ub, wk_ref[...], preferred_element_type=F32)
        vt = lax.dot_general(wvt_ref[...], ub, nt, preferred_element_type=F32).astype(BF16)
        ss = jnp.dot(ub, ws_ref[...], preferred_element_type=F32)
        outs.append((qt, kk, vt, ss))
    for hf in range(2):
        r = slice(hf * hr, (hf + 1) * hr)
        qt, kk, vt, ss = outs[hf]
        qt_ref[0, :, r] = (qt * (HEAD_DIM ** -0.5 * LOG2E)).astype(BF16)
        k_ref[0, r, :] = kk.astype(BF16)
        ones = jnp.ones((V_ROWS - HEAD_COLS, hr), BF16)
        for h in range(ATTN_HEADS):
            vt_ref[0, h, 0:HEAD_COLS, r] = vt[h * HEAD_COLS:(h + 1) * HEAD_COLS, :]
            vt_ref[0, h, HEAD_COLS:V_ROWS, r] = ones
        s_ref[0, r, :] = ss


def _in_projection(x, mod3, w_in_b):
    tm = PROJ_ROWS
    aw = ATTN_WIDTH
    wqt = jnp.transpose(w_in_b[:, 0:aw])
    wk = w_in_b[:, aw:2 * aw]
    wvt = jnp.transpose(w_in_b[:, 2 * aw:3 * aw])
    ws = w_in_b[:, 3 * aw:]
    const = lambda b, i: (0, 0)
    tposed = pl.BlockSpec((1, aw, tm), lambda b, i: (b, 0, i))
    rowblk = lambda w: pl.BlockSpec((1, tm, w), lambda b, i: (b, i, 0))
    return pl.pallas_call(
        _inproj_kernel,
        out_shape=(jax.ShapeDtypeStruct((BATCH, aw, SEQ), BF16),
                   jax.ShapeDtypeStruct((BATCH, SEQ, aw), BF16),
                   jax.ShapeDtypeStruct((BATCH, ATTN_HEADS, V_ROWS, SEQ), BF16),
                   jax.ShapeDtypeStruct((BATCH, SEQ, SSM_WIDTH), F32)),
        grid=(BATCH, SEQ // tm),
        in_specs=[rowblk(D_MODEL),
                  pl.BlockSpec((1, 1, D_MODEL), lambda b, i: (b, 0, 0)),
                  pl.BlockSpec((1, 1, D_MODEL), lambda b, i: (b, 0, 1)),
                  pl.BlockSpec((aw, D_MODEL), const), pl.BlockSpec((D_MODEL, aw), const),
                  pl.BlockSpec((aw, D_MODEL), const), pl.BlockSpec((D_MODEL, SSM_WIDTH), const)],
        out_specs=(tposed, rowblk(aw),
                   pl.BlockSpec((1, ATTN_HEADS, V_ROWS, tm), lambda b, i: (b, 0, 0, i)),
                   rowblk(SSM_WIDTH)),
        compiler_params=pltpu.CompilerParams(vmem_limit_bytes=VMEM_LIMIT_BYTES),
        name="ln_inproj",
    )(x, mod3, mod3, wqt, wk, wvt, ws)


def _t5_bucket_np(dist):
    max_exact = REL_BUCKETS // 2
    distf = np.maximum(dist, 1).astype(np.float32)
    large = max_exact + (np.log(distf / max_exact) / math.log(REL_MAX_DIST / max_exact)
                         * (REL_BUCKETS - max_exact)).astype(np.int32)
    large = np.minimum(large, REL_BUCKETS - 1)
    return np.where(dist < max_exact, dist, large).astype(np.int32)


def _bucket_tiles():
    kpos = np.arange(ATTN_TK)[:, None]
    qpos = np.arange(ATTN_TQ)[None, :]
    diag = qpos - kpos
    diag_b = np.where(diag >= 0, _t5_bucket_np(np.maximum(diag, 0)), -1)
    sub_b = _t5_bucket_np(diag + ATTN_TK)
    return np.stack([diag_b, sub_b]).astype(np.int32)


def _bias_kernel(tab_ref, bucket_ref, o_ref):
    h = pl.program_id(0)
    far = tab_ref[REL_BUCKETS - 1, h]
    for tile in range(2):
        b = bucket_ref[tile]
        acc = jnp.full(b.shape, NEG_INF, F32)
        for n in range(REL_BUCKETS):
            acc = jnp.where(b == n, (tab_ref[n, h] - far) * LOG2E, acc)
        o_ref[0, tile] = acc


def _bias_tiles(rel_bias):
    buckets = jnp.asarray(_bucket_tiles())
    return pl.pallas_call(
        _bias_kernel,
        out_shape=jax.ShapeDtypeStruct((ATTN_HEADS, 2, ATTN_TK, ATTN_TQ), F32),
        grid=(ATTN_HEADS,),
        in_specs=[pl.BlockSpec(memory_space=pltpu.SMEM),
                  pl.BlockSpec((2, ATTN_TK, ATTN_TQ), lambda h: (0, 0, 0))],
        out_specs=pl.BlockSpec((1, 2, ATTN_TK, ATTN_TQ), lambda h: (h, 0, 0, 0)),
        name="t5_bias_tiles",
    )(rel_bias, buckets)


def _attn_kernel(*refs, lam_init):
    def body(qi, carry):
        _attn_block(qi, *refs, lam_init=lam_init)
        return carry

    lax.fori_loop(0, SEQ // ATTN_TQ, body, 0)


def _attn_block(qi, qt_ref, k_ref, vt_ref, bias_ref, lq1_ref, lk1_ref, lq2_ref, lk2_ref, g_ref,
                o_ref, m_ref, acc_ref, qs_ref, *, lam_init):
    tq, tk = ATTN_TQ, ATTN_TK
    qrows = pl.ds(pl.multiple_of(qi * tq, tq), tq)

    qt = qt_ref[0, :, qrows].astype(F32)
    row = lax.broadcasted_iota(jnp.int32, qt.shape, 0)
    qs_ref[0] = jnp.where(row < HEAD_DIM, qt, 0.0).astype(BF16)
    qs_ref[1] = jnp.where(row >= HEAD_DIM, qt, 0.0).astype(BF16)

    def scores(j):
        k = k_ref[0, pl.ds(pl.multiple_of(j * tk, tk), tk), :]
        return [jnp.dot(k, qs_ref[c], preferred_element_type=F32) for c in range(2)]

    def update(j, ss):
        vt = vt_ref[0, 0, :, pl.ds(pl.multiple_of(j * tk, tk), tk)]
        for c in range(2):
            s = ss[c]
            m_prev = m_ref[c]
            m_new = jnp.maximum(m_prev, jnp.max(s, axis=0, keepdims=True))
            alpha = jnp.exp2(m_prev - m_new)
            p = jnp.exp2(s - m_new).astype(BF16)
            acc_ref[c] = alpha * acc_ref[c] + jnp.dot(vt, p, preferred_element_type=F32)
            m_ref[c] = m_new

    m_ref[...] = jnp.full(m_ref.shape, NEG_INF, F32)
    acc_ref[...] = jnp.zeros_like(acc_ref)

    nfar = jnp.maximum(qi - 1, 0)

    def far_pair(i, carry):
        s_a = scores(2 * i)
        s_b = scores(2 * i + 1)
        update(2 * i, s_a)
        update(2 * i + 1, s_b)
        return carry

    lax.fori_loop(0, nfar // 2, far_pair, 0)

    half = tk // 2

    def diag_scores():
        parts = []
        for part in range(2):
            koff = pl.multiple_of(qi * tk + part * half, half)
            k = k_ref[0, pl.ds(koff, half), :]
            q0 = part * half
            parts.append([jnp.dot(k, qs_ref[c, :, q0:tq], preferred_element_type=F32)
                          + bias_ref[0, 0, part * half:(part + 1) * half, q0:tq] for c in range(2)])
        return parts

    def diag_update(parts):
        for part in range(2):
            koff = pl.multiple_of(qi * tk + part * half, half)
            vt = vt_ref[0, 0, :, pl.ds(koff, half)]
            q0 = part * half
            for c in range(2):
                s = parts[part][c]
                m_prev = m_ref[c, :, q0:tq]
                m_new = jnp.maximum(m_prev, jnp.max(s, axis=0, keepdims=True))
                alpha = jnp.exp2(m_prev - m_new)
                p = jnp.exp2(s - m_new).astype(BF16)
                acc_ref[c, :, q0:tq] = (alpha * acc_ref[c, :, q0:tq]
                                        + jnp.dot(vt, p, preferred_element_type=F32))
                m_ref[c, :, q0:tq] = m_new

    def sub_and_diag(i, carry):
        s_sub = [s + bias_ref[0, 1] for s in scores(qi - 1)]
        parts = diag_scores()
        update(qi - 1, s_sub)
        diag_update(parts)
        return carry

    def far_sub_and_diag(i, carry):
        s_far = scores(nfar - 1)
        s_sub = [s + bias_ref[0, 1] for s in scores(qi - 1)]
        parts = diag_scores()
        update(nfar - 1, s_far)
        update(qi - 1, s_sub)
        diag_update(parts)
        return carry

    odd = nfar % 2
    lax.fori_loop(0, odd, far_sub_and_diag, 0)
    lax.fori_loop(0, jnp.minimum(qi, 1) * (1 - odd), sub_and_diag, 0)

    def diag_only(i, carry):
        diag_update(diag_scores())
        return carry

    lax.fori_loop(0, 1 - jnp.minimum(qi, 1), diag_only, 0)

    lam = (jnp.exp(jnp.sum(lq1_ref[...] * lk1_ref[...], axis=1, keepdims=True))
           - jnp.exp(jnp.sum(lq2_ref[...] * lk2_ref[...], axis=1, keepdims=True)) + lam_init)
    on = [acc_ref[c, 0:HEAD_COLS, :] / acc_ref[c, HEAD_COLS:HEAD_COLS + 1, :] for c in range(2)]
    ot = on[0] - lam * on[1]
    ms = jnp.mean(ot * ot, axis=0, keepdims=True)
    ot = ot * lax.rsqrt(ms + LN_EPS) * g_ref[...] * (1.0 - lam_init)
    o_ref[0, qrows, :] = jnp.transpose(ot).astype(BF16)


def _attention(qt, k, vt, bias_tiles, lam_q1, lam_k1, lam_q2, lam_k2, subln_g, lam_init):
    tq = ATTN_TQ
    small = lambda b, h: (0, 0)
    return pl.pallas_call(
        functools.partial(_attn_kernel, lam_init=lam_init),
        out_shape=jax.ShapeDtypeStruct((BATCH, SEQ, ATTN_WIDTH), BF16),
        grid=(BATCH, ATTN_HEADS),
        in_specs=[pl.BlockSpec((1, HEAD_COLS, SEQ), lambda b, h: (b, h, 0)),
                  pl.BlockSpec((1, SEQ, HEAD_COLS), lambda b, h: (b, 0, h)),
                  pl.BlockSpec((1, 1, V_ROWS, SEQ), lambda b, h: (b, h, 0, 0)),
                  pl.BlockSpec((1, 2, ATTN_TK, tq), lambda b, h: (h, 0, 0, 0)),
                  pl.BlockSpec((1, HEAD_DIM), small), pl.BlockSpec((1, HEAD_DIM), small),
                  pl.BlockSpec((1, HEAD_DIM), small), pl.BlockSpec((1, HEAD_DIM), small),
                  pl.BlockSpec((HEAD_COLS, 1), small)],
        out_specs=pl.BlockSpec((1, SEQ, HEAD_COLS), lambda b, h: (b, 0, h)),
        scratch_shapes=[pltpu.VMEM((2, 1, tq), F32),
                        pltpu.VMEM((2, V_ROWS, tq), F32),
                        pltpu.VMEM((2, HEAD_COLS, tq), BF16)],
        compiler_params=pltpu.CompilerParams(vmem_limit_bytes=VMEM_LIMIT_BYTES),
        name="diff_attention",
    )(qt, k, vt, bias_tiles, lam_q1, lam_k1, lam_q2, lam_k2, subln_g)


def _ssm_prep_kernel(are_ref, aim_ref, ldt_ref, ebr_ref, ebi_ref, ecr_ref, eci_ref,
                     eb_ref, ec_ref, abar_ref):
    ar = are_ref[0]
    ai = aim_ref[0]
    dt = jnp.exp(ldt_ref[0])
    mag = jnp.exp(ar * dt)
    ang = ai * dt
    pr = mag * jnp.cos(ang)
    pim = mag * jnp.sin(ang)
    den = ar * ar + ai * ai
    fr = ((pr - 1.0) * ar + pim * ai) / den
    fi = (pim * ar - (pr - 1.0) * ai) / den
    ebr = ebr_ref[0]
    ebi = ebi_ref[0]
    ns = SSM_HALF_STATE
    bre = fr * ebr - fi * ebi
    bim = fr * ebi + fi * ebr
    cre = ecr_ref[0].astype(BF16)
    cim = (-eci_ref[0]).astype(BF16)
    pows = [(jnp.ones_like(pr), jnp.zeros_like(pr)), (pr, pim)]
    for _ in range(2, SSM_BLOCK + 1):
        xr, xi = pows[-1]
        pows.append((xr * pr - xi * pim, xr * pim + xi * pr))
    for i in range(SSM_BLOCK):
        xr, xi = pows[SSM_BLOCK - 1 - i]
        blk_r = (bre * xr - bim * xi).astype(BF16)
        blk_i = (bre * xi + bim * xr).astype(BF16)
        rows = slice(i * SSM_HALF, (i + 1) * SSM_HALF)
        for k in range(ns // LANES):
            src = slice(k * LANES, (k + 1) * LANES)
            eb_ref[0, rows, 2 * k * LANES:(2 * k + 1) * LANES] = blk_r[:, src]
            eb_ref[0, rows, (2 * k + 1) * LANES:(2 * k + 2) * LANES] = blk_i[:, src]
    for k in range(ns // LANES):
        src = slice(k * LANES, (k + 1) * LANES)
        ec_ref[0, 2 * k * LANES:(2 * k + 1) * LANES, :] = cre[src, :]
        ec_ref[0, (2 * k + 1) * LANES:(2 * k + 2) * LANES, :] = cim[src, :]
    qr, qi = pr, pim
    for _ in range(int(math.log2(SSM_CHUNK))):
        qr, qi = qr * qr - qi * qi, 2.0 * qr * qi
    br, bi = pows[SSM_BLOCK]
    abar_ref[0] = jnp.concatenate([pr, pim, qr, qi, br, bi, jnp.zeros((2, ns), F32)], axis=0)


def _block_diag_half(w):
    g2 = SSM_GROUPS // 2
    w = w.reshape(2, g2, w.shape[1], w.shape[2])
    eye = jnp.eye(g2, dtype=w.dtype)
    out = w[:, :, :, None, :] * eye[None, :, None, :, None]
    return out.reshape(2, g2 * w.shape[2], g2 * w.shape[3])


def _ssm_prep(a_re, a_im, b_re, b_im, c_re, c_im, log_dt):
    ns = SSM_HALF_STATE
    row = lambda v: v.reshape(2, 1, ns)
    ldt = jnp.broadcast_to(log_dt[:, None], (SSM_GROUPS, SSM_STATE))
    ebr = _block_diag_half(jnp.transpose(b_re, (0, 2, 1)))
    ebi = _block_diag_half(jnp.transpose(b_im, (0, 2, 1)))
    ecr = _block_diag_half(jnp.transpose(c_re, (0, 2, 1)))
    eci = _block_diag_half(jnp.transpose(c_im, (0, 2, 1)))
    vec = pl.BlockSpec((1, 1, ns), lambda j: (j, 0, 0))
    return pl.pallas_call(
        _ssm_prep_kernel,
        out_shape=(jax.ShapeDtypeStruct((2, SSM_BLOCK * SSM_HALF, 2 * ns), BF16),
                   jax.ShapeDtypeStruct((2, 2 * ns, SSM_HALF), BF16),
                   jax.ShapeDtypeStruct((2, 8, ns), F32)),
        grid=(2,),
        in_specs=[vec, vec, vec,
                  pl.BlockSpec((1, SSM_HALF, ns), lambda j: (j, 0, 0)),
                  pl.BlockSpec((1, SSM_HALF, ns), lambda j: (j, 0, 0)),
                  pl.BlockSpec((1, ns, SSM_HALF), lambda j: (j, 0, 0)),
                  pl.BlockSpec((1, ns, SSM_HALF), lambda j: (j, 0, 0))],
        out_specs=(pl.BlockSpec((1, SSM_BLOCK * SSM_HALF, 2 * ns), lambda j: (j, 0, 0)),
                   pl.BlockSpec((1, 2 * ns, SSM_HALF), lambda j: (j, 0, 0)),
                   pl.BlockSpec((1, 8, ns), lambda j: (j, 0, 0))),
        name="ssm_discretise",
    )(row(a_re), row(a_im), row(ldt), ebr, ebi, ecr, eci)


def _ssm_kernel(s0_ref, s1_ref, s2_ref, s3_ref, eb_ref, ec_ref, abar_ref, d_ref, wglu_ref, bglu_ref,
                y0_ref, y1_ref, y2_ref, y3_ref, h_ref, carry_ref):
    nb, nr, nt, ns = BATCH, SSM_ROWS, SSM_CHUNK, SSM_HALF_STATE
    rows = nb * nr
    s_refs = (s0_ref, s1_ref, s2_ref, s3_ref)
    y_refs = (y0_ref, y1_ref, y2_ref, y3_ref)

    @pl.when(pl.program_id(0) == 0)
    def _():
        carry_ref[...] = jnp.zeros_like(carry_ref)

    def s_tile(t, lo, width):
        step = pl.ds(t, nr, stride=nt)
        slabs = [s_refs[k][:, step, :].reshape(rows, LANES)
                 for k in range(lo // LANES, (lo + width) // LANES)]
        return jnp.concatenate(slabs, axis=1)

    ntile = ns // LANES

    strip = 2 * LANES

    def advance(j, s_j, nsteps):
        mul_row = {1: 0, SSM_BLOCK: 4}[nsteps]
        w_rows = slice((SSM_BLOCK - nsteps) * SSM_HALF, SSM_BLOCK * SSM_HALF)
        tiles = [None] * (2 * ntile)
        for k in range(ntile):
            lo = k * LANES
            bu = jnp.dot(s_j, eb_ref[j, w_rows, k * strip:(k + 1) * strip], preferred_element_type=F32)
            ar = abar_ref[j, mul_row:mul_row + 1, lo:lo + LANES]
            ai = abar_ref[j, mul_row + 1:mul_row + 2, lo:lo + LANES]
            hr = h_ref[j, 2 * k]
            hi = h_ref[j, 2 * k + 1]
            tiles[2 * k] = ar * hr - ai * hi + bu[:, 0:LANES]
            tiles[2 * k + 1] = ar * hi + ai * hr + bu[:, LANES:strip]
            h_ref[j, 2 * k] = tiles[2 * k]
            h_ref[j, 2 * k + 1] = tiles[2 * k + 1]
        return tiles

    h_ref[...] = jnp.zeros_like(h_ref)

    def local_block(q, carry):
        for j in range(2):
            s_j = jnp.concatenate([s_tile(q * SSM_BLOCK + i, j * SSM_HALF, SSM_HALF)
                                   for i in range(SSM_BLOCK)], axis=1).astype(BF16)
            advance(j, s_j, SSM_BLOCK)
        return carry

    lax.fori_loop(0, nt // SSM_BLOCK, local_block, 0)

    for j in range(2):
        for k in range(ntile):
            lo = k * LANES
            qr = abar_ref[j, 2:3, lo:lo + LANES]
            qi = abar_ref[j, 3:4, lo:lo + LANES]
            st_r = carry_ref[j, 2 * k]
            st_i = carry_ref[j, 2 * k + 1]
            for c in range(nr):
                sel = pl.ds(c, nb, stride=nr)
                loc_r = h_ref[j, 2 * k, sel, :]
                loc_i = h_ref[j, 2 * k + 1, sel, :]
                h_ref[j, 2 * k, sel, :] = st_r
                h_ref[j, 2 * k + 1, sel, :] = st_i
                st_r, st_i = qr * st_r - qi * st_i + loc_r, qr * st_i + qi * st_r + loc_i
            carry_ref[j, 2 * k] = st_r
            carry_ref[j, 2 * k + 1] = st_i

    def output_step(t, carry):
        s_t = s_tile(t, 0, SSM_WIDTH)
        ys = []
        for j in range(2):
            tiles = advance(j, s_t[:, j * SSM_HALF:(j + 1) * SSM_HALF].astype(BF16), 1)
            ys.append(jnp.dot(jnp.concatenate(tiles, axis=1).astype(BF16), ec_ref[j],
                              preferred_element_type=F32))
        y = jnp.concatenate(ys, axis=1) + d_ref[...] * s_t
        y = jax.nn.gelu(y)
        z = jnp.dot(y.astype(BF16), wglu_ref[...], preferred_element_type=F32) + bglu_ref[...]
        out = y * jax.nn.sigmoid(z)
        step = pl.ds(t, nr, stride=nt)
        for k in range(SSM_WIDTH // LANES):
            y_refs[k][:, step, :] = out[:, k * LANES:(k + 1) * LANES].reshape(nb, nr, LANES)
        return carry

    lax.fori_loop(0, nt, output_step, 0)


def _ssm(s, eb, ec, abar, d_skip, w_glu_b, b_glu):
    nt, nr, ns = SSM_CHUNK, SSM_ROWS, SSM_HALF_STATE
    nslab = SSM_WIDTH // LANES
    const3 = lambda i: (0, 0, 0)
    const2 = lambda i: (0, 0)
    once = dict(pipeline_mode=pl.Buffered(1))
    slab = lambda k: pl.BlockSpec((BATCH, nr * nt, LANES), lambda i: (0, i, k))
    return pl.pallas_call(
        _ssm_kernel,
        out_shape=tuple(jax.ShapeDtypeStruct((BATCH, SEQ, LANES), F32) for _ in range(nslab)),
        grid=(SEQ // (nr * nt),),
        in_specs=[slab(k) for k in range(nslab)] + [
            pl.BlockSpec((2, SSM_BLOCK * SSM_HALF, 2 * ns), const3, **once),
            pl.BlockSpec((2, 2 * ns, SSM_HALF), const3, **once),
            pl.BlockSpec((2, 8, ns), const3),
            pl.BlockSpec((1, SSM_WIDTH), const2),
            pl.BlockSpec((SSM_WIDTH, SSM_WIDTH), const2),
            pl.BlockSpec((1, SSM_WIDTH), const2)],
        out_specs=tuple(slab(0) for _ in range(nslab)),
        scratch_shapes=[pltpu.VMEM((2, 2 * ns // LANES, BATCH * nr, LANES), F32),
                        pltpu.VMEM((2, 2 * ns // LANES, BATCH, LANES), F32)],
        compiler_params=pltpu.CompilerParams(dimension_semantics=("arbitrary",),
                                             vmem_limit_bytes=VMEM_LIMIT_BYTES),
        name="s5_ssm_glu",
    )(s, s, s, s, eb, ec, abar, d_skip.reshape(1, SSM_WIDTH), w_glu_b, b_glu.reshape(1, SSM_WIDTH))


def _ffn_kernel(x_ref, o_ref, y0_ref, y1_ref, y2_ref, y3_ref, g1_ref, sh2_ref, sc2_ref, g2_ref, wo_ref,
                ln1g_ref, ln1b_ref, wup_ref, cw_ref, wdn_ref, ln2g_ref, ln2b_ref, out_ref, carry_ref):
    tm = FFN_ROWS

    @pl.when(pl.program_id(1) == 0)
    def _():
        carry_ref[...] = jnp.zeros_like(carry_ref)

    y = jnp.concatenate([y0_ref[0], y1_ref[0], y2_ref[0], y3_ref[0]], axis=1).astype(BF16)
    m = (jnp.dot(o_ref[0], wo_ref[0:ATTN_WIDTH, :], preferred_element_type=F32)
         + jnp.dot(y, wo_ref[ATTN_WIDTH:, :], preferred_element_type=F32))
    x1 = _layernorm(DEEPNORM_ALPHA * x_ref[0] + g1_ref[0] * m) * ln1g_ref[...] + ln1b_ref[...]
    ub = (_layernorm(x1) * (1.0 + sc2_ref[0]) + sh2_ref[0]).astype(BF16)

    halo = [carry_ref[idx] for idx in range(2 * FFN_STEPS)]
    tails = [None] * (2 * FFN_STEPS)

    def up(idx):
        return jnp.dot(ub, wup_ref[idx], preferred_element_type=F32)

    def conv_cols(idx, hcur):
        ext = jnp.concatenate([halo[idx], hcur], axis=0)
        tails[idx] = hcur[tm - SUBLANES:tm, :]
        w = cw_ref[idx]
        return (w[2:3] * hcur + w[1:2] * ext[SUBLANES - 1:SUBLANES - 1 + tm, :]
                + w[0:1] * ext[SUBLANES - 2:SUBLANES - 2 + tm, :] + w[3:4])

    f = None
    group = []
    nxt = (up(0), up(FFN_STEPS))
    for j in range(FFN_STEPS):
        cur = nxt
        if j + 1 < FFN_STEPS:
            nxt = (up(j + 1), up(j + 1 + FFN_STEPS))
        val = conv_cols(j, cur[0])
        gate = conv_cols(j + FFN_STEPS, cur[1])
        group.append((gate * jax.nn.sigmoid(gate) * val).astype(BF16))
        if len(group) < FFN_GROUP and j + 1 < FFN_STEPS:
            continue
        lo = (j + 1 - len(group)) * FFN_COLS
        a = jnp.concatenate(group, axis=1) if len(group) > 1 else group[0]
        part = jnp.dot(a, wdn_ref[lo:(j + 1) * FFN_COLS, :], preferred_element_type=F32)
        f = part if f is None else f + part
        group = []
    for idx in range(2 * FFN_STEPS):
        carry_ref[idx] = tails[idx]
    out_ref[0] = (_layernorm(DEEPNORM_ALPHA * x1 + g2_ref[0] * f) * ln2g_ref[...] + ln2b_ref[...])


def _outproj_ffn(x, o, y_slabs, mod3, w_out_b, ln1_g, ln1_b, w_up_c, conv_c, w_down_c, ln2_g, ln2_b):
    tm = FFN_ROWS
    row = lambda b, i: (b, i, 0)
    modv = lambda k: pl.BlockSpec((1, 1, D_MODEL), lambda b, i: (b, 0, k))
    c2 = lambda b, i: (0, 0)
    c3 = lambda b, i: (0, 0, 0)
    vec = pl.BlockSpec((1, D_MODEL), c2)
    return pl.pallas_call(
        _ffn_kernel,
        out_shape=jax.ShapeDtypeStruct((BATCH, SEQ, D_MODEL), F32),
        grid=(BATCH, SEQ // tm),
        in_specs=[pl.BlockSpec((1, tm, D_MODEL), row),
                  pl.BlockSpec((1, tm, ATTN_WIDTH), row),
                  *[pl.BlockSpec((1, tm, LANES), row) for _ in range(SSM_WIDTH // LANES)],
                  modv(2), modv(3), modv(4), modv(5),
                  pl.BlockSpec((D_MODEL, D_MODEL), c2, pipeline_mode=pl.Buffered(1)),
                  vec, vec,
                  pl.BlockSpec((2 * FFN_STEPS, D_MODEL, FFN_COLS), c3, pipeline_mode=pl.Buffered(1)),
                  pl.BlockSpec((2 * FFN_STEPS, SUBLANES, FFN_COLS), c3),
                  pl.BlockSpec((D_FF, D_MODEL), c2, pipeline_mode=pl.Buffered(1)),
                  vec, vec],
        out_specs=pl.BlockSpec((1, tm, D_MODEL), row),
        scratch_shapes=[pltpu.VMEM((2 * FFN_STEPS, SUBLANES, FFN_COLS), F32)],
        compiler_params=pltpu.CompilerParams(dimension_semantics=("arbitrary", "arbitrary"),
                                             vmem_limit_bytes=VMEM_LIMIT_BYTES),
        name="outproj_ffn",
    )(x, o, *y_slabs, mod3, mod3, mod3, mod3, w_out_b, ln1_g, ln1_b, w_up_c, conv_c, w_down_c, ln2_g, ln2_b)


def kernel(x, c, w_ada, b_ada, w_in, lam_q1, lam_k1, lam_q2, lam_k2, subln_g, ssm_a_re, ssm_a_im,
           ssm_b_re, ssm_b_im, ssm_c_re, ssm_c_im, ssm_d, ssm_log_dt, w_glu, b_glu, w_out, ln1_g,
           ln1_b, w_up, conv_w, conv_b, w_down, ln2_g, ln2_b, rel_bias):
    bias_tiles = _bias_tiles(rel_bias)
    for layer in range(DEPTH):
        lam_init = 0.8 - 0.6 * math.exp(-0.3 * layer)
        mod3 = _modulation(c, w_ada[layer], b_ada[layer]).reshape(BATCH, 1, 6 * D_MODEL)
        qt, k, vt, s = _in_projection(x, mod3, w_in[layer].astype(BF16))
        o = _attention(qt, k, vt, bias_tiles, lam_q1[layer][None], lam_k1[layer][None],
                       lam_q2[layer][None], lam_k2[layer][None], subln_g[layer][:, None], lam_init)
        eb, ec, abar = _ssm_prep(ssm_a_re[layer], ssm_a_im[layer], ssm_b_re[layer], ssm_b_im[layer],
                                 ssm_c_re[layer], ssm_c_im[layer], ssm_log_dt[layer])
        y = _ssm(s, eb, ec, abar, ssm_d[layer], w_glu[layer].astype(BF16), b_glu[layer])
        w_up_c = jnp.transpose(w_up[layer].astype(BF16).reshape(D_MODEL, 2 * FFN_STEPS, FFN_COLS),
                               (1, 0, 2))
        conv_c = jnp.concatenate([conv_w[layer], conv_b[layer][None],
                                  jnp.zeros((SUBLANES - 4, 2 * D_FF), F32)], axis=0)
        conv_c = jnp.transpose(conv_c.reshape(SUBLANES, 2 * FFN_STEPS, FFN_COLS), (1, 0, 2))
        x = _outproj_ffn(x, o, y, mod3, w_out[layer].astype(BF16), ln1_g[layer][None],
                         ln1_b[layer][None], w_up_c, conv_c, w_down[layer].astype(BF16),
                         ln2_g[layer][None],
                         ln2_b[layer][None])
    return x
```
